```python
import math
import jax
import jax.numpy as jnp
from jax import lax
import numpy as np

D_MODEL = 1024
BATCH = 8
SEQ = 4096
DEPTH = 2
DEC_BATCH = 32
DEC_SEQ = 64
PAST_LEN = 4096

CHUNK = 64
Q_BLOCK = 128
ROPE_THETA = 10000.0
EPS = 1e-6
MACARON_W = 0.5

MLA_HEADS = 16
MLA_NOPE = 64
MLA_ROPE = 32
MLA_V = 64
MLA_KV_RANK = 256
MLA_Q_RANK = 512
MLA_SCALE = (MLA_NOPE + MLA_ROPE) ** -0.5

DIFF_HEADS = 8
DIFF_HEAD_DIM = 64
DIFF_V_DIM = 2 * DIFF_HEAD_DIM
DIFF_SCALE = DIFF_HEAD_DIM ** -0.5

CONV_CH = D_MODEL
CONV_WIDTH = 31
CONV_STATE = CONV_WIDTH - 1

D_FF = 2816
N_BRANCH = 3
N_ADA = 9

IN_COLS = (MLA_Q_RANK, MLA_KV_RANK, MLA_ROPE,
           DIFF_HEADS * 2 * DIFF_HEAD_DIM, DIFF_HEADS * 2 * DIFF_HEAD_DIM, DIFF_HEADS * DIFF_V_DIM,
           2 * CONV_CH)
IN_SPLITS = tuple(int(s) for s in np.cumsum(IN_COLS)[:-1])
D_IN = int(sum(IN_COLS))

kernel_name = 'hybrid_chunk_causal_encoder_step'


def rmsnorm(x, g):
    xf = x.astype(jnp.float32)
    y = xf * lax.rsqrt(jnp.mean(xf * xf, axis=-1, keepdims=True) + EPS)
    return (y * g.astype(jnp.float32)).astype(x.dtype)


def layernorm(x, g, b):
    xf = x.astype(jnp.float32)
    mu = jnp.mean(xf, axis=-1, keepdims=True)
    var = jnp.mean(jnp.square(xf - mu), axis=-1, keepdims=True)
    y = (xf - mu) * lax.rsqrt(var + EPS)
    return (y * g.astype(jnp.float32) + b.astype(jnp.float32)).astype(x.dtype)


def rope(x, pos):
    half = x.shape[-1] // 2
    inv_freq = ROPE_THETA ** (-jnp.arange(half, dtype=jnp.float32) / half)
    ang = pos.astype(jnp.float32)[:, None] * inv_freq[None, :]
    ang = ang.reshape((pos.shape[0],) + (1,) * (x.ndim - 3) + (half,))
    cos, sin = jnp.cos(ang), jnp.sin(ang)
    xf = x.astype(jnp.float32)
    x1, x2 = xf[..., :half], xf[..., half:]
    return jnp.concatenate([x1 * cos - x2 * sin, x2 * cos + x1 * sin], axis=-1).astype(x.dtype)


def chunk_mask(q_pos, k_pos):
    return (k_pos[None, :] // CHUNK) <= (q_pos[:, None] // CHUNK)


def over_query_blocks(fn, q_args, q_pos):
    T = q_pos.shape[0]
    if T > Q_BLOCK and T % Q_BLOCK == 0:
        nb = T // Q_BLOCK
        qb = tuple(jnp.moveaxis(a.reshape((a.shape[0], nb, Q_BLOCK) + a.shape[2:]), 1, 0) for a in q_args)
        out = lax.map(lambda t: fn(t[0], t[1]), (qb, q_pos.reshape(nb, Q_BLOCK)))
        out = jnp.moveaxis(out, 0, 1)
        return out.reshape((out.shape[0], T) + out.shape[3:])
    return fn(q_args, q_pos)


def swiglu(h, wg, wu, wd):
    return (jax.nn.silu(h @ wg) * (h @ wu)) @ wd


def token_mixer(u, past, p, layer):
    B, T, _ = u.shape
    past_len = 0 if past is None else past[0].shape[1]
    q_pos = past_len + jnp.arange(T, dtype=jnp.int32)
    k_pos = jnp.arange(past_len + T, dtype=jnp.int32)
    neg = jnp.finfo(jnp.float32).min
    cq, ckv, kpe, dq, dk, dv, cv = jnp.split(u @ p['w_in'], IN_SPLITS, axis=-1)

    cq = rmsnorm(cq, p['mla_q_norm_g'])
    q = (cq @ p['mla_w_uq']).reshape(B, T, MLA_HEADS, MLA_NOPE + MLA_ROPE)
    q_nope = q[..., :MLA_NOPE]
    q_pe = rope(q[..., MLA_NOPE:], q_pos)
    ckv = rmsnorm(ckv, p['mla_kv_norm_g'])
    kpe = rope(kpe, q_pos)
    ckv_all = ckv if past is None else jnp.concatenate([past[0], ckv], axis=1)
    kpe_all = kpe if past is None else jnp.concatenate([past[1], kpe], axis=1)
    Tk = ckv_all.shape[1]
    k_nope = (ckv_all @ p['mla_w_uk']).reshape(B, Tk, MLA_HEADS, MLA_NOPE)
    v_mla = (ckv_all @ p['mla_w_uv']).reshape(B, Tk, MLA_HEADS, MLA_V)

    def mla_block(qb, pb):
        qn, qr = qb
        s = (jnp.einsum('bqhd,bkhd->bhqk', qn, k_nope)
             + jnp.einsum('bqhr,bkr->bhqk', qr, kpe_all)).astype(jnp.float32) * MLA_SCALE
        s = jnp.where(chunk_mask(pb, k_pos), s, neg)
        pr = jax.nn.softmax(s, axis=-1).astype(v_mla.dtype)
        return jnp.einsum('bhqk,bkhd->bqhd', pr, v_mla)

    o_mla = over_query_blocks(mla_block, (q_nope, q_pe), q_pos)
    o_mla = o_mla.reshape(B, T, MLA_HEADS * MLA_V) @ p['mla_w_o']

    dq = rope(dq.reshape(B, T, DIFF_HEADS, 2, DIFF_HEAD_DIM), q_pos)
    dk = rope(dk.reshape(B, T, DIFF_HEADS, 2, DIFF_HEAD_DIM), q_pos)
    dv = dv.reshape(B, T, DIFF_HEADS, DIFF_V_DIM)
    dk_all = dk if past is None else jnp.concatenate([past[2], dk], axis=1)
    dv_all = dv if past is None else jnp.concatenate([past[3], dv], axis=1)
    lam_init = 0.8 - 0.6 * math.exp(-0.3 * layer)
    lam = (jnp.exp(jnp.sum((p['diff_lq1'] * p['diff_lk1']).astype(jnp.float32)))
           - jnp.exp(jnp.sum((p['diff_lq2'] * p['diff_lk2']).astype(jnp.float32))) + lam_init)

    def diff_block(qb, pb):
        qd = qb[0]
        s = jnp.einsum('bqhmd,bkhmd->bmhqk', qd, dk_all).astype(jnp.float32) * DIFF_SCALE
        s = jnp.where(chunk_mask(pb, k_pos), s, neg)
        pr = jax.nn.softmax(s, axis=-1)
        a = (pr[:, 0] - lam * pr[:, 1]).astype(dv_all.dtype)
        return jnp.einsum('bhqk,bkhe->bqhe', a, dv_all)

    o_diff = over_query_blocks(diff_block, (dq,), q_pos)
    o_diff = rmsnorm(o_diff, p['diff_subln_g']) * (1.0 - lam_init)
    o_diff = o_diff.reshape(B, T, DIFF_HEADS * DIFF_V_DIM) @ p['diff_w_o']

    ga, gb = jnp.split(cv, 2, axis=-1)
    glu = ga * jax.nn.sigmoid(gb)
    prev = jnp.zeros((B, CONV_STATE, CONV_CH), glu.dtype) if past is None else past[4]
    xin = jnp.concatenate([prev, glu], axis=1)
    y = lax.conv_general_dilated(xin, p['conv_w_dw'][:, None, :], window_strides=(1,), padding='VALID',
                                 dimension_numbers=('NWC', 'WIO', 'NWC'), feature_group_count=CONV_CH)
    y = jax.nn.silu(layernorm(y + p['conv_b_dw'], p['conv_ln_g'], p['conv_ln_b']))
    o_conv = y @ p['conv_w_pw2'] + p['conv_b_pw2']
    new_conv = xin[:, -CONV_STATE:]

    gates = jax.nn.sigmoid(u @ p['w_branch_gate'] + p['b_branch_gate']).reshape(B, T, N_BRANCH, D_MODEL)
    merged = gates[:, :, 0] * o_mla + gates[:, :, 1] * o_diff + gates[:, :, 2] * o_conv
    return merged @ p['w_out'], (ckv, kpe, dk, dv, new_conv)


def layer_forward(x, c, past, p, layer):
    mods = (jax.nn.silu(c) @ p['ada_w'] + p['ada_b'])[:, None, :]
    sh1, sc1, g1, sh2, sc2, g2, sh3, sc3, g3 = jnp.split(mods, N_ADA, axis=-1)
    h = rmsnorm(x, p['ffn1_pre_g']) * (1.0 + sc1) + sh1
    x = x + MACARON_W * g1 * rmsnorm(swiglu(h, p['ffn1_w_gate'], p['ffn1_w_up'], p['ffn1_w_down']), p['ffn1_post_g'])
    h = rmsnorm(x, p['mix_pre_g']) * (1.0 + sc2) + sh2
    o, state = token_mixer(h, past, p, layer)
    x = x + g2 * rmsnorm(o, p['mix_post_g'])
    h = rmsnorm(x, p['ffn2_pre_g']) * (1.0 + sc3) + sh3
    x = x + MACARON_W * g3 * rmsnorm(swiglu(h, p['ffn2_w_gate'], p['ffn2_w_up'], p['ffn2_w_down']), p['ffn2_post_g'])
    return x, state


def setup_inputs(seed: int = 0) -> dict:
    key = jax.random.key(seed)
    ks = iter(jax.random.split(key, 64))
    L, D = DEPTH, D_MODEL

    def nrm(shape, scale):
        return jax.random.normal(next(ks), shape, jnp.float32) * scale

    def gain(shape):
        return 1.0 + nrm(shape, 0.02)

    return {
        'x_prompt': nrm((BATCH, SEQ, D), 1.0),
        'x_sample': nrm((DEC_BATCH, DEC_SEQ, D), 1.0),
        'c_prompt': nrm((BATCH, D), 1.0),
        'c_sample': nrm((DEC_BATCH, D), 1.0),
        'cache_mla_ckv': nrm((L, DEC_BATCH, PAST_LEN, MLA_KV_RANK), 1.0),
        'cache_mla_kpe': nrm((L, DEC_BATCH, PAST_LEN, MLA_ROPE), 1.0),
        'cache_diff_k': nrm((L, DEC_BATCH, PAST_LEN, DIFF_HEADS, 2, DIFF_HEAD_DIM), 1.0),
        'cache_diff_v': nrm((L, DEC_BATCH, PAST_LEN, DIFF_HEADS, DIFF_V_DIM), 1.0),
        'state_conv': nrm((L, DEC_BATCH, CONV_STATE, CONV_CH), 0.5),
        'ada_w': nrm((L, D, N_ADA * D), D ** -0.5),
        'ada_b': nrm((L, N_ADA * D), 0.02),
        'ffn1_pre_g': gain((L, D)),
        'ffn1_post_g': gain((L, D)),
        'ffn1_w_gate': nrm((L, D, D_FF), D ** -0.5),
        'ffn1_w_up': nrm((L, D, D_FF), D ** -0.5),
        'ffn1_w_down': nrm((L, D_FF, D), D_FF ** -0.5),
        'mix_pre_g': gain((L, D)),
        'mix_post_g': gain((L, D)),
        'w_in': nrm((L, D, D_IN), D ** -0.5),
        'mla_q_norm_g': gain((L, MLA_Q_RANK)),
        'mla_w_uq': nrm((L, MLA_Q_RANK, MLA_HEADS * (MLA_NOPE + MLA_ROPE)), MLA_Q_RANK ** -0.5),
        'mla_kv_norm_g': gain((L, MLA_KV_RANK)),
        'mla_w_uk': nrm((L, MLA_KV_RANK, MLA_HEADS * MLA_NOPE), MLA_KV_RANK ** -0.5),
        'mla_w_uv': nrm((L, MLA_KV_RANK, MLA_HEADS * MLA_V), MLA_KV_RANK ** -0.5),
        'mla_w_o': nrm((L, MLA_HEADS * MLA_V, D), (MLA_HEADS * MLA_V) ** -0.5),
        'diff_lq1': nrm((L, DIFF_HEAD_DIM), 0.1),
        'diff_lk1': nrm((L, DIFF_HEAD_DIM), 0.1),
        'diff_lq2': nrm((L, DIFF_HEAD_DIM), 0.1),
        'diff_lk2': nrm((L, DIFF_HEAD_DIM), 0.1),
        'diff_subln_g': gain((L, DIFF_V_DIM)),
        'diff_w_o': nrm((L, DIFF_HEADS * DIFF_V_DIM, D), (DIFF_HEADS * DIFF_V_DIM) ** -0.5),
        'conv_w_dw': nrm((L, CONV_WIDTH, CONV_CH), CONV_WIDTH ** -0.5),
        'conv_b_dw': nrm((L, CONV_CH), 0.02),
        'conv_ln_g': gain((L, CONV_CH)),
        'conv_ln_b': nrm((L, CONV_CH), 0.02),
        'conv_w_pw2': nrm((L, CONV_CH, D), CONV_CH ** -0.5),
        'conv_b_pw2': nrm((L, D), 0.02),
        'w_branch_gate': nrm((L, D, N_BRANCH * D), D ** -0.5),
        'b_branch_gate': nrm((L, N_BRANCH * D), 0.02),
        'w_out': nrm((L, D, D), D ** -0.5),
        'ffn2_pre_g': gain((L, D)),
        'ffn2_post_g': gain((L, D)),
        'ffn2_w_gate': nrm((L, D, D_FF), D ** -0.5),
        'ffn2_w_up': nrm((L, D, D_FF), D ** -0.5),
        'ffn2_w_down': nrm((L, D_FF, D), D_FF ** -0.5),
    }


def reference(x_prompt, x_sample, c_prompt, c_sample, cache_mla_ckv, cache_mla_kpe, cache_diff_k, cache_diff_v,
              state_conv, ada_w, ada_b, ffn1_pre_g, ffn1_post_g, ffn1_w_gate, ffn1_w_up, ffn1_w_down,
              mix_pre_g, mix_post_g, w_in, mla_q_norm_g, mla_w_uq, mla_kv_norm_g, mla_w_uk, mla_w_uv, mla_w_o,
              diff_lq1, diff_lk1, diff_lq2, diff_lk2, diff_subln_g, diff_w_o,
              conv_w_dw, conv_b_dw, conv_ln_g, conv_ln_b, conv_w_pw2, conv_b_pw2,
              w_branch_gate, b_branch_gate, w_out,
              ffn2_pre_g, ffn2_post_g, ffn2_w_gate, ffn2_w_up, ffn2_w_down):
    y_prompt, y_sample = x_prompt, x_sample
    st_p, st_s = [], []
    for l in range(DEPTH):
        p = {
            'ada_w': ada_w[l], 'ada_b': ada_b[l],
            'ffn1_pre_g': ffn1_pre_g[l], 'ffn1_post_g': ffn1_post_g[l],
            'ffn1_w_gate': ffn1_w_gate[l], 'ffn1_w_up': ffn1_w_up[l], 'ffn1_w_down': ffn1_w_down[l],
            'mix_pre_g': mix_pre_g[l], 'mix_post_g': mix_post_g[l], 'w_in': w_in[l],
            'mla_q_norm_g': mla_q_norm_g[l], 'mla_w_uq': mla_w_uq[l], 'mla_kv_norm_g': mla_kv_norm_g[l],
            'mla_w_uk': mla_w_uk[l], 'mla_w_uv': mla_w_uv[l], 'mla_w_o': mla_w_o[l],
            'diff_lq1': diff_lq1[l], 'diff_lk1': diff_lk1[l], 'diff_lq2': diff_lq2[l], 'diff_lk2': diff_lk2[l],
            'diff_subln_g': diff_subln_g[l], 'diff_w_o': diff_w_o[l],
            'conv_w_dw': conv_w_dw[l], 'conv_b_dw': conv_b_dw[l], 'conv_ln_g': conv_ln_g[l],
            'conv_ln_b': conv_ln_b[l], 'conv_w_pw2': conv_w_pw2[l], 'conv_b_pw2': conv_b_pw2[l],
            'w_branch_gate': w_branch_gate[l], 'b_branch_gate': b_branch_gate[l], 'w_out': w_out[l],
            'ffn2_pre_g': ffn2_pre_g[l], 'ffn2_post_g': ffn2_post_g[l],
            'ffn2_w_gate': ffn2_w_gate[l], 'ffn2_w_up': ffn2_w_up[l], 'ffn2_w_down': ffn2_w_down[l],
        }
        y_prompt, sp = layer_forward(y_prompt, c_prompt, None, p, l)
        past = (cache_mla_ckv[l], cache_mla_kpe[l], cache_diff_k[l], cache_diff_v[l], state_conv[l])
        y_sample, ss = layer_forward(y_sample, c_sample, past, p, l)
        st_p.append(sp)
        st_s.append(ss)
    new_mla_ckv_prompt = jnp.stack([s[0] for s in st_p])
    new_mla_kpe_prompt = jnp.stack([s[1] for s in st_p])
    new_diff_k_prompt = jnp.stack([s[2] for s in st_p])
    new_diff_v_prompt = jnp.stack([s[3] for s in st_p])
    new_conv_prompt = jnp.stack([s[4] for s in st_p])
    new_mla_ckv_sample = jnp.stack([s[0] for s in st_s])
    new_mla_kpe_sample = jnp.stack([s[1] for s in st_s])
    new_diff_k_sample = jnp.stack([s[2] for s in st_s])
    new_diff_v_sample = jnp.stack([s[3] for s in st_s])
    new_conv_sample = jnp.stack([s[4] for s in st_s])
    return (y_prompt, y_sample,
            new_mla_ckv_prompt, new_mla_kpe_prompt, new_diff_k_prompt, new_diff_v_prompt, new_conv_prompt,
            new_mla_ckv_sample, new_mla_kpe_sample, new_diff_k_sample, new_diff_v_sample, new_conv_sample)
```

```python
import functools
import math

import jax
import jax.numpy as jnp
from jax import lax
from jax.experimental import pallas as pl
from jax.experimental.pallas import tpu as pltpu

D_MODEL = 1024
DEPTH = 2
CHUNK = 64
ROPE_THETA = 10000.0
EPS = 1e-6
MACARON_W = 0.5

MLA_HEADS = 16
MLA_NOPE = 64
MLA_ROPE = 32
MLA_V = 64
MLA_KV_RANK = 256
MLA_Q_RANK = 512
MLA_SCALE = (MLA_NOPE + MLA_ROPE) ** -0.5

DIFF_HEADS = 8
DIFF_HEAD_DIM = 64
DIFF_V_DIM = 2 * DIFF_HEAD_DIM
DIFF_SCALE = DIFF_HEAD_DIM ** -0.5

CONV_WIDTH = 31
CONV_STATE = CONV_WIDTH - 1
CONV_HALO = 32
D_FF = 2816
N_ADA = 9

LANES = 128
SUBLANES = 8
HEAD_SLOT = LANES
MLA_PAD = MLA_HEADS * HEAD_SLOT
DIFF_W = DIFF_HEADS * 2 * DIFF_HEAD_DIM
MIB = 1024 * 1024

C_CQ = 0
C_CKV = C_CQ + MLA_Q_RANK
C_KPE = C_CKV + MLA_KV_RANK
C_DQ = C_KPE + HEAD_SLOT
C_DK = C_DQ + DIFF_W
C_DV = C_DK + DIFF_W
C_GA = C_DV + DIFF_W
C_GB = C_GA + D_MODEL
C_END = C_GB + D_MODEL

F32 = jnp.float32
BF16 = jnp.bfloat16


def _cparams(sem, vmem_mib):
    return pltpu.CompilerParams(dimension_semantics=sem, vmem_limit_bytes=int(vmem_mib * MIB))


def _dot(a, b):
    return jnp.dot(a, b, preferred_element_type=F32)


def _dot_nt(a, b):
    return lax.dot_general(a, b, (((1,), (1,)), ((), ())), preferred_element_type=F32)


def _rms(x):
    return x * lax.rsqrt(jnp.mean(x * x, axis=-1, keepdims=True) + EPS)


def _mod_norm(x, g, mod, nseq):
    tm = x.shape[0]
    xn = _rms(x) * g
    sh = mod[:, 0:1, :]
    sc = mod[:, 1:2, :]
    if nseq == 1:
        return xn * (1.0 + sc[0]) + sh[0]
    xn = xn.reshape(nseq, tm // nseq, x.shape[1])
    return (xn * (1.0 + sc) + sh).reshape(tm, x.shape[1])


def _gated_residual(x, y, post_g, mod, nseq, weight):
    tm = x.shape[0]
    yn = _rms(y) * post_g
    gate = mod[:, 2:3, :]
    if nseq == 1:
        return x + weight * gate[0] * yn
    yn = yn.reshape(nseq, tm // nseq, x.shape[1])
    return x + (weight * gate * yn).reshape(tm, x.shape[1])


def _mod_spec(tm, srows):
    if tm <= srows:
        per = srows // tm
        return 1, pl.BlockSpec((1, 3, D_MODEL), lambda *idx: (idx[0] // per, 0, 0))
    nseq = tm // srows
    return nseq, pl.BlockSpec((nseq, 3, D_MODEL), lambda *idx: (idx[0], 0, 0))


def _const_spec(shape):
    nd = len(shape)
    return pl.BlockSpec(shape, lambda *idx: (0,) * nd, pipeline_mode=pl.Buffered(1))


def _ada_kernel(c_ref, w_ref, b_ref, o_ref):
    c = c_ref[...]
    a = (c * jax.nn.sigmoid(c)).astype(BF16)
    o_ref[0] = _dot(a, w_ref[0].astype(BF16)) + b_ref[0]


def _ada(c_all, ada_w, ada_b):
    nb = c_all.shape[0]
    tn = 1024
    n = N_ADA * D_MODEL
    return pl.pallas_call(
        _ada_kernel,
        grid=(DEPTH, n // tn),
        in_specs=[pl.BlockSpec((nb, D_MODEL), lambda l, j: (0, 0)),
                  pl.BlockSpec((1, D_MODEL, tn), lambda l, j: (l, 0, j)),
                  pl.BlockSpec((1, 1, tn), lambda l, j: (l, 0, j))],
        out_specs=pl.BlockSpec((1, nb, tn), lambda l, j: (l, 0, j)),
        out_shape=jax.ShapeDtypeStruct((DEPTH, nb, n), F32),
        compiler_params=_cparams(("parallel", "parallel"), 24),
        name="ada",
    )(c_all, ada_w, ada_b.reshape(DEPTH, 1, n))


def _ffn_kernel(x_ref, mod_ref, preg_ref, postg_ref, wg_ref, wu_ref, wd_ref, o_ref, h_ref, acc_ref, *, nseq):
    j = pl.program_id(1)

    @pl.when(j == 0)
    def _():
        h = _mod_norm(x_ref[...], preg_ref[...], mod_ref[...], nseq)
        h_ref[...] = h.astype(BF16)
        acc_ref[...] = jnp.zeros_like(acc_ref)

    h = h_ref[...]
    g = _dot(h, wg_ref[...])
    u = _dot(h, wu_ref[...])
    a = (g * jax.nn.sigmoid(g) * u).astype(BF16)
    acc_ref[...] += _dot(a, wd_ref[...])

    @pl.when(j == pl.num_programs(1) - 1)
    def _():
        o_ref[...] = _gated_residual(x_ref[...], acc_ref[...], postg_ref[...], mod_ref[...], nseq, MACARON_W)


def _ffn(x, mod, pre_g, post_g, wg, wu, wd, *, srows, tm=512, tf=1408):
    rows = x.shape[0]
    nseq, mod_spec = _mod_spec(tm, srows)
    row_spec = pl.BlockSpec((tm, D_MODEL), lambda i, j: (i, 0))
    vec_spec = pl.BlockSpec((1, D_MODEL), lambda i, j: (0, 0))
    return pl.pallas_call(
        functools.partial(_ffn_kernel, nseq=nseq),
        grid=(rows // tm, D_FF // tf),
        in_specs=[row_spec, mod_spec, vec_spec, vec_spec,
                  pl.BlockSpec((D_MODEL, tf), lambda i, j: (0, j)),
                  pl.BlockSpec((D_MODEL, tf), lambda i, j: (0, j)),
                  pl.BlockSpec((tf, D_MODEL), lambda i, j: (j, 0))],
        out_specs=row_spec,
        out_shape=jax.ShapeDtypeStruct((rows, D_MODEL), F32),
        scratch_shapes=[pltpu.VMEM((tm, D_MODEL), BF16), pltpu.VMEM((tm, D_MODEL), F32)],
        compiler_params=_cparams(("parallel", "arbitrary"), 56),
        name="ffn",
    )(x, mod, pre_g, post_g, wg, wu, wd)


def _rope_store(x, tbl_ref, shift, out_refs):
    n = x.shape[1]
    c, s1, s2 = tbl_ref[0], tbl_ref[1], tbl_ref[2]
    xm = pltpu.roll(x, n - shift, 1)
    xp = pltpu.roll(x, shift, 1)
    for b in range(n // LANES):
        sl = slice(b * LANES, (b + 1) * LANES)
        r = x[:, sl] * c + xm[:, sl] * s1 + xp[:, sl] * s2
        for ref in out_refs:
            ref[:, sl] = r.astype(ref.dtype)


def _mix_in_kernel(x_ref, mod_ref, preg_ref, win_ref, qg_ref, wuq_ref, kvg_ref, wuk_ref, wuv_ref,
                   tmq_ref, tmk_ref, tdq_ref, tdk_ref,
                   qpad_ref, ckv_ref, kpe_ref, dq_ref, dkf_ref, dkb_ref, dvf_ref, dvb_ref, glu_ref,
                   *kv_refs, nseq):
    u = _mod_norm(x_ref[...], preg_ref[...], mod_ref[...], nseq).astype(BF16)

    def proj(a, b):
        return _dot(u, win_ref[:, a:b])

    cqn = (_rms(proj(C_CQ, C_CKV)) * qg_ref[...]).astype(BF16)
    _rope_store(_dot(cqn, wuq_ref[...]), tmq_ref, MLA_ROPE // 2, (qpad_ref,))

    ckvn = _rms(proj(C_CKV, C_KPE)) * kvg_ref[...]
    ckv_ref[...] = ckvn
    _rope_store(proj(C_KPE, C_DQ), tmk_ref, MLA_ROPE // 2, (kpe_ref,))

    if kv_refs:
        kpad_ref, vpad_ref = kv_refs
        ckvb = ckvn.astype(BF16)
        kpe = kpe_ref[...]
        kn = _dot(ckvb, wuk_ref[...])
        for h in range(MLA_HEADS):
            sl = slice(h * HEAD_SLOT, (h + 1) * HEAD_SLOT)
            kpad_ref[:, sl] = (kn[:, sl] + kpe).astype(BF16)
        v = _dot(ckvb, wuv_ref[...])
        low = lax.broadcasted_iota(jnp.int32, (1, LANES), 1) < MLA_V
        for p in range(MLA_HEADS // 2):
            vp = v[:, p * LANES:(p + 1) * LANES]
            vpad_ref[:, (2 * p) * LANES:(2 * p + 1) * LANES] = jnp.where(low, vp, 0.0).astype(BF16)
            vpad_ref[:, (2 * p + 1) * LANES:(2 * p + 2) * LANES] = jnp.where(low, 0.0, vp).astype(BF16)

    _rope_store(proj(C_DQ, C_DK), tdq_ref, DIFF_HEAD_DIM // 2, (dq_ref,))
    _rope_store(proj(C_DK, C_DV), tdk_ref, DIFF_HEAD_DIM // 2, (dkf_ref, dkb_ref))
    dv = proj(C_DV, C_GA)
    dvf_ref[...] = dv
    dvb_ref[...] = dv.astype(BF16)

    glu_ref[...] = proj(C_GA, C_GB) * jax.nn.sigmoid(proj(C_GB, C_END))


def _mix_in(x, mod, pre_g, w_in, q_g, w_uq, kv_g, w_uk, w_uv, tables, *, srows, with_kv, tm=256):
    rows = x.shape[0]
    nseq, mod_spec = _mod_spec(tm, srows)
    ntab = tables[0].shape[1] // tm
    row = lambda n: pl.BlockSpec((tm, n), lambda i: (i, 0))
    tab_spec = pl.BlockSpec((3, tm, LANES), lambda i: (0, i % ntab, 0))
    out_cols = [(MLA_PAD, BF16), (MLA_KV_RANK, F32), (LANES, F32), (DIFF_W, BF16), (DIFF_W, F32), (DIFF_W, BF16),
                (DIFF_W, F32), (DIFF_W, BF16), (D_MODEL, F32)]
    if with_kv:
        out_cols += [(MLA_PAD, BF16), (MLA_PAD, BF16)]
    return pl.pallas_call(
        functools.partial(_mix_in_kernel, nseq=nseq),
        grid=(rows // tm,),
        in_specs=[row(D_MODEL), mod_spec, _const_spec((1, D_MODEL)), _const_spec((D_MODEL, C_END)),
                  _const_spec((1, MLA_Q_RANK)), _const_spec((MLA_Q_RANK, MLA_PAD)),
                  _const_spec((1, MLA_KV_RANK)), _const_spec((MLA_KV_RANK, MLA_PAD)),
                  _const_spec((MLA_KV_RANK, MLA_HEADS * MLA_V)),
                  tab_spec, tab_spec, tab_spec, tab_spec],
        out_specs=[row(n) for n, _ in out_cols],
        out_shape=[jax.ShapeDtypeStruct((rows, n), dt) for n, dt in out_cols],
        compiler_params=_cparams(("parallel",), 56),
        name="mix_in",
    )(x, mod, pre_g, w_in, q_g, w_uq, kv_g, w_uk, w_uv, *tables)


def _chunk_mask(tq, tk):
    qc = lax.broadcasted_iota(jnp.int32, (tq, tk), 0) // CHUNK
    kc = lax.broadcasted_iota(jnp.int32, (tq, tk), 1) // CHUNK
    return kc <= qc


def _online_step(s, m_ref, l_ref):
    m_prev = m_ref[...]
    m_new = jnp.maximum(m_prev, jnp.max(s, axis=-1, keepdims=True))
    alpha = jnp.exp(m_prev - m_new)
    p = jnp.exp(s - m_new)
    l_ref[...] = alpha * l_ref[...] + jnp.sum(p, axis=-1, keepdims=True)
    m_ref[...] = m_new
    return p, alpha


def _mla_attn_kernel(q_ref, k_ref, v_ref, o_ref, m_ref, l_ref, acc_ref, *, tq):
    i = pl.program_id(2)
    m_ref[...] = jnp.full_like(m_ref, -jnp.inf)
    l_ref[...] = jnp.zeros_like(l_ref)
    acc_ref[...] = jnp.zeros_like(acc_ref)
    low = lax.broadcasted_iota(jnp.int32, (1, LANES), 1) < MLA_V

    def step(j, masked):
        rows = pl.ds(pl.multiple_of(j * tq, tq), tq)
        for hh in range(2):
            sl = slice(hh * HEAD_SLOT, (hh + 1) * HEAD_SLOT)
            s = _dot_nt(q_ref[:, sl], k_ref[rows, sl])
            if masked:
                s = jnp.where(_chunk_mask(tq, tq), s, -jnp.inf)
            p, alpha = _online_step(s, m_ref.at[hh], l_ref.at[hh])
            pv = _dot(p.astype(BF16), v_ref[rows, sl])
            mine = low if hh == 0 else jnp.logical_not(low)
            acc_ref[...] = acc_ref[...] * jnp.where(mine, alpha, 1.0) + pv

    def body(j, carry):
        step(j, False)
        return carry

    lax.fori_loop(0, i, body, 0)
    step(i, True)
    o_ref[...] = (acc_ref[...] / jnp.where(low, l_ref[0], l_ref[1])).astype(o_ref.dtype)


def _mla_attn_prompt(q_pad, k_pad, v_pad, *, batch, seq, tq=512):
    nq = seq // tq
    pair = 2 * HEAD_SLOT
    return pl.pallas_call(
        functools.partial(_mla_attn_kernel, tq=tq),
        grid=(batch, MLA_HEADS // 2, nq),
        in_specs=[pl.BlockSpec((tq, pair), lambda b, p, i: (b * nq + i, p)),
                  pl.BlockSpec((seq, pair), lambda b, p, i: (b, p)),
                  pl.BlockSpec((seq, pair), lambda b, p, i: (b, p))],
        out_specs=pl.BlockSpec((tq, LANES), lambda b, p, i: (b * nq + i, p)),
        out_shape=jax.ShapeDtypeStruct((batch * seq, MLA_HEADS * MLA_V), BF16),
        scratch_shapes=[pltpu.VMEM((2, tq, 1), F32), pltpu.VMEM((2, tq, 1), F32), pltpu.VMEM((tq, LANES), F32)],
        compiler_params=_cparams(("parallel", "parallel", "arbitrary"), 40),
        name="mla_attn",
    )(q_pad, k_pad, v_pad)


def _diff_lambda(lam_ref, lam_init):
    lp = lam_ref[...]
    s1 = jnp.sum(lp[0:1] * lp[1:2], axis=-1, keepdims=True)
    s2 = jnp.sum(lp[2:3] * lp[3:4], axis=-1, keepdims=True)
    return jnp.exp(s1) - jnp.exp(s2) + lam_init


def _diff_finish(o, g_ref, lam_init):
    return _rms(o) * g_ref[...] * (1.0 - lam_init)


def _split_maps(q):
    low = lax.broadcasted_iota(jnp.int32, (1, LANES), 1) < DIFF_HEAD_DIM
    zero = jnp.zeros_like(q)
    return jnp.where(low, q, zero), jnp.where(low, zero, q)


def _diff_attn_kernel(q_ref, k_ref, v_ref, lam_ref, g_ref, o_ref, m_ref, l_ref, acc_ref, *, tq, lam_init):
    i = pl.program_id(2)
    m_ref[...] = jnp.full_like(m_ref, -jnp.inf)
    l_ref[...] = jnp.zeros_like(l_ref)
    acc_ref[...] = jnp.zeros_like(acc_ref)
    qs = _split_maps(q_ref[...])

    def step(j, masked):
        rows = pl.ds(pl.multiple_of(j * tq, tq), tq)
        k = k_ref[rows, :]
        v = v_ref[rows, :]
        for mp in range(2):
            s = _dot_nt(qs[mp], k)
            if masked:
                s = jnp.where(_chunk_mask(tq, tq), s, -jnp.inf)
            p, alpha = _online_step(s, m_ref.at[mp], l_ref.at[mp])
            acc_ref[mp] = acc_ref[mp] * alpha + _dot(p.astype(BF16), v)

    def body(j, carry):
        step(j, False)
        return carry

    lax.fori_loop(0, i, body, 0)
    step(i, True)
    lam = _diff_lambda(lam_ref, lam_init)
    o = acc_ref[0] / l_ref[0] - lam * (acc_ref[1] / l_ref[1])
    o_ref[...] = _diff_finish(o, g_ref, lam_init).astype(o_ref.dtype)


def _diff_attn_prompt(dq, dk, dv, lam_p, sub_g, *, batch, seq, lam_init, tq=512):
    nq = seq // tq
    return pl.pallas_call(
        functools.partial(_diff_attn_kernel, tq=tq, lam_init=lam_init),
        grid=(batch, DIFF_HEADS, nq),
        in_specs=[pl.BlockSpec((tq, LANES), lambda b, h, i: (b * nq + i, h)),
                  pl.BlockSpec((seq, LANES), lambda b, h, i: (b, h)),
                  pl.BlockSpec((seq, LANES), lambda b, h, i: (b, h)),
                  pl.BlockSpec((4, DIFF_HEAD_DIM), lambda b, h, i: (0, 0)),
                  pl.BlockSpec((1, DIFF_V_DIM), lambda b, h, i: (0, 0))],
        out_specs=pl.BlockSpec((tq, LANES), lambda b, h, i: (b * nq + i, h)),
        out_shape=jax.ShapeDtypeStruct((batch * seq, DIFF_W), BF16),
        scratch_shapes=[pltpu.VMEM((2, tq, 1), F32), pltpu.VMEM((2, tq, 1), F32), pltpu.VMEM((2, tq, LANES), F32)],
        compiler_params=_cparams(("parallel", "parallel", "arbitrary"), 40),
        name="diff_attn",
    )(dq, dk, dv, lam_p, sub_g)


def _mla_sample_kernel(q_ref, ckvp_ref, kpep_ref, ckvn_ref, kpen_ref, wuk_ref, wuvp_ref, o_ref, *, group):
    ckv_p = ckvp_ref[0].astype(BF16)
    past = ckv_p.shape[0]
    kpe_p = jnp.concatenate([kpep_ref[0], jnp.zeros((past, LANES - MLA_ROPE), F32)], axis=1).astype(BF16)
    ckv_n = ckvn_ref[...].astype(BF16)
    kpe_n = kpen_ref[...].astype(BF16)
    t = q_ref.shape[0]
    for g0 in range(0, MLA_HEADS, group):
        qlat, qfull, qpe = [], [], []
        for h in range(g0, g0 + group):
            sl = slice(h * HEAD_SLOT, (h + 1) * HEAD_SLOT)
            qh = q_ref[:, sl]
            qlat.append(_dot_nt(qh, wuk_ref[:, sl]).astype(BF16))
            qfull.append(qh)
            qpe.append(jnp.concatenate([qh[:, MLA_NOPE:MLA_NOPE + MLA_ROPE],
                                        jnp.zeros((t, LANES - MLA_ROPE), BF16)], axis=1))
        qlat = jnp.concatenate(qlat, axis=0)
        qfull = jnp.concatenate(qfull, axis=0)
        qpe = jnp.concatenate(qpe, axis=0)
        s_p = _dot_nt(qlat, ckv_p) + _dot_nt(qpe, kpe_p)
        s_n = _dot_nt(qlat, ckv_n) + _dot_nt(qfull, kpe_n)
        m = jnp.maximum(jnp.max(s_p, axis=-1, keepdims=True), jnp.max(s_n, axis=-1, keepdims=True))
        p_p = jnp.exp(s_p - m)
        p_n = jnp.exp(s_n - m)
        l = jnp.sum(p_p, axis=-1, keepdims=True) + jnp.sum(p_n, axis=-1, keepdims=True)
        o_lat = ((_dot(p_p.astype(BF16), ckv_p) + _dot(p_n.astype(BF16), ckv_n)) / l).astype(BF16)
        for a in range(0, group, 2):
            h = g0 + a
            o_pair = (_dot(o_lat[a * t:(a + 1) * t], wuvp_ref[:, h * HEAD_SLOT:(h + 1) * HEAD_SLOT])
                      + _dot(o_lat[(a + 1) * t:(a + 2) * t], wuvp_ref[:, (h + 1) * HEAD_SLOT:(h + 2) * HEAD_SLOT]))
            o_ref[:, (h // 2) * LANES:(h // 2 + 1) * LANES] = o_pair.astype(o_ref.dtype)


def _mla_attn_sample(q_pad, ckv_past, kpe_past, ckv_new, kpe_new, w_uk, w_uv_pad, *, batch, t):
    past = ckv_past.shape[1]
    return pl.pallas_call(
        functools.partial(_mla_sample_kernel, group=4),
        grid=(batch,),
        in_specs=[pl.BlockSpec((t, MLA_PAD), lambda b: (b, 0)),
                  pl.BlockSpec((1, past, MLA_KV_RANK), lambda b: (b, 0, 0)),
                  pl.BlockSpec((1, past, MLA_ROPE), lambda b: (b, 0, 0)),
                  pl.BlockSpec((t, MLA_KV_RANK), lambda b: (b, 0)),
                  pl.BlockSpec((t, LANES), lambda b: (b, 0)),
                  _const_spec((MLA_KV_RANK, MLA_PAD)),
                  _const_spec((MLA_KV_RANK, MLA_PAD))],
        out_specs=pl.BlockSpec((t, MLA_HEADS * MLA_V), lambda b: (b, 0)),
        out_shape=jax.ShapeDtypeStruct((batch * t, MLA_HEADS * MLA_V), BF16),
        compiler_params=_cparams(("parallel",), 48),
        name="mla_attn_sample",
    )(q_pad, ckv_past, kpe_past, ckv_new, kpe_new, w_uk, w_uv_pad)


def _diff_sample_kernel(q_ref, kp_ref, vp_ref, kn_ref, vn_ref, lam_ref, g_ref, o_ref, *, lam_init):
    t = q_ref.shape[0]
    q0, q1 = _split_maps(q_ref[...])
    qm = jnp.concatenate([q0, q1], axis=0)
    s_p = _dot_nt(qm, kp_ref[0].astype(BF16))
    s_n = _dot_nt(qm, kn_ref[...])
    m = jnp.maximum(jnp.max(s_p, axis=-1, keepdims=True), jnp.max(s_n, axis=-1, keepdims=True))
    p_p = jnp.exp(s_p - m)
    p_n = jnp.exp(s_n - m)
    inv = 1.0 / (jnp.sum(p_p, axis=-1, keepdims=True) + jnp.sum(p_n, axis=-1, keepdims=True))
    lam = _diff_lambda(lam_ref, lam_init)
    w0 = inv[:t]
    w1 = lam * inv[t:]
    a_p = (p_p[:t] * w0 - p_p[t:] * w1).astype(BF16)
    a_n = (p_n[:t] * w0 - p_n[t:] * w1).astype(BF16)
    o = _dot(a_p, vp_ref[0].astype(BF16)) + _dot(a_n, vn_ref[...])
    o_ref[...] = _diff_finish(o, g_ref, lam_init).astype(o_ref.dtype)


def _diff_attn_sample(dq, k_past, v_past, dk_new, dv_new, lam_p, sub_g, *, batch, t, lam_init):
    past = k_past.shape[1]
    new = pl.BlockSpec((t, LANES), lambda b, h: (b, h))
    cache = pl.BlockSpec((1, past, LANES), lambda b, h: (b, 0, h))
    return pl.pallas_call(
        functools.partial(_diff_sample_kernel, lam_init=lam_init),
        grid=(batch, DIFF_HEADS),
        in_specs=[new, cache, cache, new, new,
                  pl.BlockSpec((4, DIFF_HEAD_DIM), lambda b, h: (0, 0)),
                  pl.BlockSpec((1, DIFF_V_DIM), lambda b, h: (0, 0))],
        out_specs=new,
        out_shape=jax.ShapeDtypeStruct((batch * t, DIFF_W), BF16),
        compiler_params=_cparams(("parallel", "parallel"), 40),
        name="diff_attn_sample",
    )(dq, k_past, v_past, dk_new, dv_new, lam_p, sub_g)


def _conv_kernel(cur_ref, prev_ref, wdw_ref, bdw_ref, lng_ref, lnb_ref, o_ref, xin_ref, *, rc):
    tm = cur_ref.shape[0]
    xin_ref[0:CONV_HALO, :] = prev_ref[0]
    xin_ref[CONV_HALO:CONV_HALO + tm, :] = cur_ref[...]
    xin_ref[CONV_HALO + tm:, :] = jnp.zeros((SUBLANES, D_MODEL), F32)
    first = CONV_HALO - CONV_STATE

    def body(r, carry):
        r0 = pl.multiple_of(r * rc, rc)
        y = jnp.zeros((rc, D_MODEL), F32) + bdw_ref[...]
        for s in range(SUBLANES):
            z = jnp.zeros((rc + SUBLANES, D_MODEL), F32)
            for w in range(CONV_WIDTH):
                if (first + w) % SUBLANES == s:
                    start = pl.multiple_of(r0 + (first + w - s), SUBLANES)
                    z = z + xin_ref[pl.ds(start, rc + SUBLANES), :] * wdw_ref[w:w + 1, :]
            y = y + z[s:s + rc]
        mu = jnp.mean(y, axis=-1, keepdims=True)
        yc = y - mu
        var = jnp.mean(yc * yc, axis=-1, keepdims=True)
        yn = yc * lax.rsqrt(var + EPS) * lng_ref[...] + lnb_ref[...]
        o_ref[pl.ds(r0, rc), :] = (yn * jax.nn.sigmoid(yn)).astype(o_ref.dtype)
        return carry

    lax.fori_loop(0, tm // rc, body, 0)


def _conv(glu, prev, w_dw, b_dw, ln_g, ln_b, *, tm):
    rows = glu.shape[0]
    vec = pl.BlockSpec((1, D_MODEL), lambda i: (0, 0))
    return pl.pallas_call(
        functools.partial(_conv_kernel, rc=16),
        grid=(rows // tm,),
        in_specs=[pl.BlockSpec((tm, D_MODEL), lambda i: (i, 0)),
                  pl.BlockSpec((1, CONV_HALO, D_MODEL), lambda i: (i, 0, 0)),
                  pl.BlockSpec((CONV_WIDTH, D_MODEL), lambda i: (0, 0)),
                  vec, vec, vec],
        out_specs=pl.BlockSpec((tm, D_MODEL), lambda i: (i, 0)),
        out_shape=jax.ShapeDtypeStruct((rows, D_MODEL), BF16),
        scratch_shapes=[pltpu.VMEM((tm + CONV_HALO + SUBLANES, D_MODEL), F32)],
        compiler_params=_cparams(("parallel",), 24),
        name="conv",
    )(glu, prev, w_dw, b_dw, ln_g, ln_b)


def _merge_kernel(x_ref, mod_ref, preg_ref, postg_ref, am_ref, ad_ref, yc_ref,
                  wbg_ref, bbg_ref, wmo_ref, wdo_ref, wpw_ref, bpw_ref, wout_ref, o_ref, *, nseq):
    x = x_ref[...]
    mod = mod_ref[...]
    u = _mod_norm(x, preg_ref[...], mod, nseq).astype(BF16)
    branches = (_dot(am_ref[...], wmo_ref[...]),
                _dot(ad_ref[...], wdo_ref[...]),
                _dot(yc_ref[...], wpw_ref[...]) + bpw_ref[...])
    merged = None
    for k, br in enumerate(branches):
        sl = slice(k * D_MODEL, (k + 1) * D_MODEL)
        gate = jax.nn.sigmoid(_dot(u, wbg_ref[:, sl]) + bbg_ref[:, sl])
        merged = gate * br if merged is None else merged + gate * br
    o = _dot(merged.astype(BF16), wout_ref[...])
    o_ref[...] = _gated_residual(x, o, postg_ref[...], mod, nseq, 1.0)


def _merge(x, mod, pre_g, post_g, a_mla, a_diff, y_conv, w_bg, b_bg, w_mo, w_do, w_pw, b_pw, w_out, *, srows, tm=512):
    rows = x.shape[0]
    nseq, mod_spec = _mod_spec(tm, srows)
    row = pl.BlockSpec((tm, D_MODEL), lambda i: (i, 0))
    sq = _const_spec((D_MODEL, D_MODEL))
    vec = _const_spec((1, D_MODEL))
    return pl.pallas_call(
        functools.partial(_merge_kernel, nseq=nseq),
        grid=(rows // tm,),
        in_specs=[row, mod_spec, vec, vec, row, row, row,
                  _const_spec((D_MODEL, 3 * D_MODEL)), _const_spec((1, 3 * D_MODEL)), sq, sq, sq, vec, sq],
        out_specs=row,
        out_shape=jax.ShapeDtypeStruct((rows, D_MODEL), F32),
        compiler_params=_cparams(("parallel",), 56),
        name="merge",
    )(x, mod, pre_g, post_g, a_mla, a_diff, y_conv, w_bg, b_bg, w_mo, w_do, w_pw, b_pw, w_out)


def _rope_angles(pos, rot_dim):
    half = rot_dim // 2
    inv_freq = ROPE_THETA ** (-jnp.arange(half, dtype=F32) / half)
    ang = pos.astype(F32)[:, None] * inv_freq[None, :]
    return jnp.cos(ang), jnp.sin(ang)


def _mla_tables(pos, scale):
    cos, sin = _rope_angles(pos, MLA_ROPE)
    t = pos.shape[0]
    z = lambda n: jnp.zeros((t, n), F32)
    pad = HEAD_SLOT - MLA_NOPE - MLA_ROPE
    c = jnp.concatenate([jnp.ones((t, MLA_NOPE), F32), cos, cos, z(pad)], axis=1)
    s1 = jnp.concatenate([z(MLA_NOPE), -sin, z(MLA_ROPE // 2), z(pad)], axis=1)
    s2 = jnp.concatenate([z(MLA_NOPE), z(MLA_ROPE // 2), sin, z(pad)], axis=1)
    return jnp.stack([c, s1, s2]) * scale


def _diff_tables(pos, scale):
    cos, sin = _rope_angles(pos, DIFF_HEAD_DIM)
    z = jnp.zeros_like(sin)
    c = jnp.concatenate([cos, cos, cos, cos], axis=1)
    s1 = jnp.concatenate([-sin, z, -sin, z], axis=1)
    s2 = jnp.concatenate([z, sin, z, sin], axis=1)
    return jnp.stack([c, s1, s2]) * scale


def _tables(pos, reps):
    tabs = (_mla_tables(pos, MLA_SCALE), _mla_tables(pos, 1.0), _diff_tables(pos, DIFF_SCALE), _diff_tables(pos, 1.0))
    return tuple(jnp.tile(t, (1, reps, 1)) for t in tabs)


def _pad_heads(w, heads, width):
    k = w.shape[0]
    w = w.reshape(k, heads, width)
    return jnp.pad(w, ((0, 0), (0, 0), (0, HEAD_SLOT - width))).reshape(k, heads * HEAD_SLOT)


def _pad_pairs(w):
    k = w.shape[0]
    w = w.reshape(k, MLA_HEADS // 2, 2, MLA_V)
    z = jnp.zeros_like(w[:, :, 0])
    lo = jnp.concatenate([w[:, :, 0], z], axis=-1)
    hi = jnp.concatenate([z, w[:, :, 1]], axis=-1)
    return jnp.stack([lo, hi], axis=2).reshape(k, MLA_PAD)


def _permute_w_in(w):
    s0 = MLA_Q_RANK + MLA_KV_RANK
    z = lambda n: jnp.zeros((w.shape[0], n), w.dtype)
    return jnp.concatenate([w[:, :s0], z(MLA_NOPE), w[:, s0:s0 + MLA_ROPE], z(HEAD_SLOT - MLA_NOPE - MLA_ROPE),
                            w[:, s0 + MLA_ROPE:]], axis=1)


def _conv_prev_prompt(glu, batch, seq, tm):
    g = glu.reshape(batch, seq // tm, tm, D_MODEL)[:, :-1, tm - CONV_HALO:, :]
    g = jnp.pad(g, ((0, 0), (1, 0), (0, 0), (0, 0)))
    return g.reshape(batch * (seq // tm), CONV_HALO, D_MODEL)


def kernel(x_prompt, x_sample, c_prompt, c_sample, cache_mla_ckv, cache_mla_kpe, cache_diff_k, cache_diff_v, state_conv, ada_w, ada_b, ffn1_pre_g, ffn1_post_g, ffn1_w_gate, ffn1_w_up, ffn1_w_down, mix_pre_g, mix_post_g, w_in, mla_q_norm_g, mla_w_uq, mla_kv_norm_g, mla_w_uk, mla_w_uv, mla_w_o, diff_lq1, diff_lk1, diff_lq2, diff_lk2, diff_subln_g, diff_w_o, conv_w_dw, conv_b_dw, conv_ln_g, conv_ln_b, conv_w_pw2, conv_b_pw2, w_branch_gate, b_branch_gate, w_out, ffn2_pre_g, ffn2_post_g, ffn2_w_gate, ffn2_w_up, ffn2_w_down):
    pb, pt, _ = x_prompt.shape
    sb, st, _ = x_sample.shape
    past = cache_mla_ckv.shape[2]
    assert st == CHUNK and past % CHUNK == 0, "sample rows must be exactly the chunk that follows the cache"

    mods = _ada(jnp.concatenate([c_prompt, c_sample], axis=0), ada_w, ada_b)
    mods = mods.reshape(DEPTH, pb + sb, 3, 3, D_MODEL)

    mix_tm = 256
    tabs_p = _tables(jnp.arange(pt, dtype=jnp.int32), 1)
    tabs_s = _tables(past + jnp.arange(st, dtype=jnp.int32), mix_tm // st)

    xs = {"p": x_prompt.reshape(pb * pt, D_MODEL), "s": x_sample.reshape(sb * st, D_MODEL)}
    srows = {"p": pt, "s": st}
    states = {"p": [], "s": []}
    bf = lambda a: a.astype(BF16)
    vec = lambda a: a.reshape(1, -1)

    for l in range(DEPTH):
        lam_init = 0.8 - 0.6 * math.exp(-0.3 * l)
        w = dict(
            f1=(vec(ffn1_pre_g[l]), vec(ffn1_post_g[l]), bf(ffn1_w_gate[l]), bf(ffn1_w_up[l]), bf(ffn1_w_down[l])),
            f2=(vec(ffn2_pre_g[l]), vec(ffn2_post_g[l]), bf(ffn2_w_gate[l]), bf(ffn2_w_up[l]), bf(ffn2_w_down[l])),
            w_in=bf(_permute_w_in(w_in[l])),
            w_uq=bf(_pad_heads(mla_w_uq[l], MLA_HEADS, MLA_NOPE + MLA_ROPE)),
            w_uk=bf(_pad_heads(mla_w_uk[l], MLA_HEADS, MLA_NOPE)),
            w_uv=bf(mla_w_uv[l]),
            w_uv_pad=bf(_pad_pairs(mla_w_uv[l])),
            lam=jnp.stack([diff_lq1[l], diff_lk1[l], diff_lq2[l], diff_lk2[l]]),
        )
        for key in ("p", "s"):
            x = xs[key]
            sr = srows[key]
            mod = mods[l, :pb] if key == "p" else mods[l, pb:]
            x = _ffn(x, mod[:, 0], *w["f1"], srows=sr)
            outs = _mix_in(x, mod[:, 1], vec(mix_pre_g[l]), w["w_in"], vec(mla_q_norm_g[l]), w["w_uq"],
                           vec(mla_kv_norm_g[l]), w["w_uk"], w["w_uv"], tabs_p if key == "p" else tabs_s,
                           srows=sr, with_kv=(key == "p"), tm=mix_tm)
            q_pad, ckv, kpe, dq, dk_f, dk_b, dv_f, dv_b, glu = outs[:9]
            if key == "p":
                k_pad, v_pad = outs[9:]
                a_mla = _mla_attn_prompt(q_pad, k_pad, v_pad, batch=pb, seq=pt)
                a_diff = _diff_attn_prompt(dq, dk_b, dv_b, w["lam"], vec(diff_subln_g[l]),
                                           batch=pb, seq=pt, lam_init=lam_init)
                conv_tm = 256
                prev = _conv_prev_prompt(glu, pb, pt, conv_tm)
                new_conv = glu.reshape(pb, pt, D_MODEL)[:, pt - CONV_STATE:, :]
                nb, nt = pb, pt
            else:
                a_mla = _mla_attn_sample(q_pad, cache_mla_ckv[l], cache_mla_kpe[l], ckv, kpe, w["w_uk"],
                                         w["w_uv_pad"], batch=sb, t=st)
                a_diff = _diff_attn_sample(dq, cache_diff_k[l].reshape(sb, past, DIFF_W),
                                           cache_diff_v[l].reshape(sb, past, DIFF_W), dk_b, dv_b, w["lam"],
                                           vec(diff_subln_g[l]), batch=sb, t=st, lam_init=lam_init)
                conv_tm = st
                prev = jnp.pad(state_conv[l], ((0, 0), (CONV_HALO - CONV_STATE, 0), (0, 0)))
                xin = jnp.concatenate([state_conv[l], glu.reshape(sb, st, D_MODEL)], axis=1)
                new_conv = xin[:, st:, :]
                nb, nt = sb, st
            y_conv = _conv(glu, prev, conv_w_dw[l], vec(conv_b_dw[l]), vec(conv_ln_g[l]), vec(conv_ln_b[l]),
                           tm=conv_tm)
            x = _merge(x, mod[:, 1], vec(mix_pre_g[l]), vec(mix_post_g[l]), a_mla, a_diff, y_conv,
                       bf(w_branch_gate[l]), vec(b_branch_gate[l]), bf(mla_w_o[l]), bf(diff_w_o[l]),
                       bf(conv_w_pw2[l]), vec(conv_b_pw2[l]), bf(w_out[l]), srows=sr)
            x = _ffn(x, mod[:, 2], *w["f2"], srows=sr)
            xs[key] = x
            states[key].append((
                ckv.reshape(nb, nt, MLA_KV_RANK),
                kpe[:, MLA_NOPE:MLA_NOPE + MLA_ROPE].reshape(nb, nt, MLA_ROPE),
                dk_f.reshape(nb, nt, DIFF_HEADS, 2, DIFF_HEAD_DIM),
                dv_f.reshape(nb, nt, DIFF_HEADS, DIFF_V_DIM),
                new_conv))

    stack = lambda key, k: jnp.stack([s[k] for s in states[key]])
    return (xs["p"].reshape(pb, pt, D_MODEL), xs["s"].reshape(sb, st, D_MODEL),
            *(stack("p", k) for k in range(5)), *(stack("s", k) for k in range(5)))
```

```python
import functools
import math

import jax
import jax.numpy as jnp
from jax import lax
from jax.experimental import pallas as pl
from jax.experimental.pallas import tpu as pltpu

D_MODEL = 1024
DEPTH = 2
CHUNK = 64
ROPE_THETA = 10000.0
EPS = 1e-6
MACARON_W = 0.5

MLA_HEADS = 16
MLA_NOPE = 64
MLA_ROPE = 32
MLA_V = 64
MLA_KV_RANK = 256
MLA_Q_RANK = 512
MLA_SCALE = (MLA_NOPE + MLA_ROPE) ** -0.5

DIFF_HEADS = 8
DIFF_HEAD_DIM = 64
DIFF_V_DIM = 2 * DIFF_HEAD_DIM
DIFF_SCALE = DIFF_HEAD_DIM ** -0.5
LOG2E = math.log2(math.e)

CONV_WIDTH = 31
CONV_STATE = CONV_WIDTH - 1
CONV_HALO = 32
D_FF = 2816
N_ADA = 9

LANES = 128
SUBLANES = 8
HEAD_SLOT = LANES
MLA_PAD = MLA_HEADS * HEAD_SLOT
DIFF_W = DIFF_HEADS * 2 * DIFF_HEAD_DIM
MIB = 1024 * 1024

C_CQ = 0
C_CKV = C_CQ + MLA_Q_RANK
C_KPE = C_CKV + MLA_KV_RANK
C_DQ = C_KPE + HEAD_SLOT
C_DK = C_DQ + DIFF_W
C_DV = C_DK + DIFF_W
C_GA = C_DV + DIFF_W
C_GB = C_GA + D_MODEL
C_END = C_GB + D_MODEL

F32 = jnp.float32
BF16 = jnp.bfloat16


def _cparams(sem, vmem_mib):
    return pltpu.CompilerParams(dimension_semantics=sem, vmem_limit_bytes=int(vmem_mib * MIB))


def _dot(a, b):
    return jnp.dot(a, b, preferred_element_type=F32)


def _dot_nt(a, b):
    return lax.dot_general(a, b, (((1,), (1,)), ((), ())), preferred_element_type=F32)


def _rms(x):
    return x * lax.rsqrt(jnp.mean(x * x, axis=-1, keepdims=True) + EPS)


def _mod_norm(x, g, mod, nseq):
    tm = x.shape[0]
    xn = _rms(x) * g
    sh = mod[:, 0:1, :]
    sc = mod[:, 1:2, :]
    if nseq == 1:
        return xn * (1.0 + sc[0]) + sh[0]
    xn = xn.reshape(nseq, tm // nseq, x.shape[1])
    return (xn * (1.0 + sc) + sh).reshape(tm, x.shape[1])


def _gated_residual(x, y, post_g, mod, nseq, weight):
    tm = x.shape[0]
    yn = _rms(y) * post_g
    gate = mod[:, 2:3, :]
    if nseq == 1:
        return x + weight * gate[0] * yn
    yn = yn.reshape(nseq, tm // nseq, x.shape[1])
    return x + (weight * gate * yn).reshape(tm, x.shape[1])


def _mod_spec(tm, srows):
    if tm <= srows:
        per = srows // tm
        return 1, pl.BlockSpec((1, 3, D_MODEL), lambda *idx: (idx[0] // per, 0, 0))
    nseq = tm // srows
    return nseq, pl.BlockSpec((nseq, 3, D_MODEL), lambda *idx: (idx[0], 0, 0))


def _const_spec(shape):
    nd = len(shape)
    return pl.BlockSpec(shape, lambda *idx: (0,) * nd, pipeline_mode=pl.Buffered(1))


def _ada_kernel(c_ref, w_ref, b_ref, o_ref):
    c = c_ref[...]
    a = (c * jax.nn.sigmoid(c)).astype(BF16)
    o_ref[0] = _dot(a, w_ref[0].astype(BF16)) + b_ref[0]


def _ada(c_all, ada_w, ada_b):
    nb = c_all.shape[0]
    tn = 1024
    n = N_ADA * D_MODEL
    return pl.pallas_call(
        _ada_kernel,
        grid=(DEPTH, n // tn),
        in_specs=[pl.BlockSpec((nb, D_MODEL), lambda l, j: (0, 0)),
                  pl.BlockSpec((1, D_MODEL, tn), lambda l, j: (l, 0, j)),
                  pl.BlockSpec((1, 1, tn), lambda l, j: (l, 0, j))],
        out_specs=pl.BlockSpec((1, nb, tn), lambda l, j: (l, 0, j)),
        out_shape=jax.ShapeDtypeStruct((DEPTH, nb, n), F32),
        compiler_params=_cparams(("parallel", "parallel"), 24),
        name="ada",
    )(c_all, ada_w, ada_b.reshape(DEPTH, 1, n))


def _ffn_kernel(x_ref, mod_ref, preg_ref, postg_ref, wg_ref, wu_ref, wd_ref, o_ref, h_ref, acc_ref, *, nseq):
    j = pl.program_id(1)

    @pl.when(j == 0)
    def _():
        h = _mod_norm(x_ref[...], preg_ref[...], mod_ref[...], nseq)
        h_ref[...] = h.astype(BF16)
        acc_ref[...] = jnp.zeros_like(acc_ref)

    h = h_ref[...]
    g = _dot(h, wg_ref[...])
    u = _dot(h, wu_ref[...])
    a = (g * jax.nn.sigmoid(g) * u).astype(BF16)
    acc_ref[...] += _dot(a, wd_ref[...])

    @pl.when(j == pl.num_programs(1) - 1)
    def _():
        o_ref[...] = _gated_residual(x_ref[...], acc_ref[...], postg_ref[...], mod_ref[...], nseq, MACARON_W)


def _ffn(x, mod, pre_g, post_g, wg, wu, wd, *, srows, tm=512, tf=1408):
    rows = x.shape[0]
    nseq, mod_spec = _mod_spec(tm, srows)
    row_spec = pl.BlockSpec((tm, D_MODEL), lambda i, j: (i, 0))
    vec_spec = pl.BlockSpec((1, D_MODEL), lambda i, j: (0, 0))
    return pl.pallas_call(
        functools.partial(_ffn_kernel, nseq=nseq),
        grid=(rows // tm, D_FF // tf),
        in_specs=[row_spec, mod_spec, vec_spec, vec_spec,
                  pl.BlockSpec((D_MODEL, tf), lambda i, j: (0, j)),
                  pl.BlockSpec((D_MODEL, tf), lambda i, j: (0, j)),
                  pl.BlockSpec((tf, D_MODEL), lambda i, j: (j, 0))],
        out_specs=row_spec,
        out_shape=jax.ShapeDtypeStruct((rows, D_MODEL), F32),
        scratch_shapes=[pltpu.VMEM((tm, D_MODEL), BF16), pltpu.VMEM((tm, D_MODEL), F32)],
        compiler_params=_cparams(("parallel", "arbitrary"), 56),
        name="ffn",
    )(x, mod, pre_g, post_g, wg, wu, wd)


def _rope_store(x, tbl_ref, shift, out_refs, t_ref=None):
    n = x.shape[1]
    c, s1, s2 = tbl_ref[0], tbl_ref[1], tbl_ref[2]
    xm = pltpu.roll(x, n - shift, 1)
    xp = pltpu.roll(x, shift, 1)
    for b in range(n // LANES):
        sl = slice(b * LANES, (b + 1) * LANES)
        r = x[:, sl] * c + xm[:, sl] * s1 + xp[:, sl] * s2
        for ref in out_refs:
            ref[:, sl] = r.astype(ref.dtype)
        if t_ref is not None:
            t_ref[sl, :] = r.T


N_MIX_IN = 13


def _mix_in_kernel(*refs, nseq, prompt):
    (x_ref, mod_ref, preg_ref, win_ref, qg_ref, wuq_ref, kvg_ref, wuk_ref, wuv_ref,
     tmq_ref, tmk_ref, tdq_ref, tdk_ref) = refs[:N_MIX_IN]
    outs = refs[len(refs) - (11 if prompt else 9):]
    qpad_ref, ckv_ref, kpe_ref, dq_ref, dkb_ref, dvb_ref, glu_ref = outs[:7]
    if prompt:
        kpad_ref, vpad_ref, dkt_ref, dvs_ref = outs[7:]
        dk_outs, dk_t, dvf_ref = (dkb_ref,), dkt_ref.at[0, 0], dvs_ref.at[0]
    else:
        dkf_ref, dvf_ref = outs[7:]
        dk_outs, dk_t = (dkf_ref, dkb_ref), None
    u = _mod_norm(x_ref[...], preg_ref[...], mod_ref[...], nseq).astype(BF16)

    def proj(a, b):
        return _dot(u, win_ref[:, a:b])

    cqn = (_rms(proj(C_CQ, C_CKV)) * qg_ref[...]).astype(BF16)
    _rope_store(_dot(cqn, wuq_ref[...]), tmq_ref, MLA_ROPE // 2, (qpad_ref,))

    ckvn = _rms(proj(C_CKV, C_KPE)) * kvg_ref[...]
    ckv_ref[...] = ckvn
    _rope_store(proj(C_KPE, C_DQ), tmk_ref, MLA_ROPE // 2, (kpe_ref,))

    if prompt:
        ckvb = ckvn.astype(BF16)
        kpe = kpe_ref[...]
        kn = _dot(ckvb, wuk_ref[...])
        for h in range(MLA_HEADS):
            sl = slice(h * HEAD_SLOT, (h + 1) * HEAD_SLOT)
            kpad_ref[:, sl] = (kn[:, sl] + kpe).astype(BF16)
        v = _dot(ckvb, wuv_ref[...])
        low = lax.broadcasted_iota(jnp.int32, (1, LANES), 1) < MLA_V
        for p in range(MLA_HEADS // 2):
            vp = v[:, p * LANES:(p + 1) * LANES]
            vpad_ref[:, (2 * p) * LANES:(2 * p + 1) * LANES] = jnp.where(low, vp, 0.0).astype(BF16)
            vpad_ref[:, (2 * p + 1) * LANES:(2 * p + 2) * LANES] = jnp.where(low, 0.0, vp).astype(BF16)

    _rope_store(proj(C_DQ, C_DK), tdq_ref, DIFF_HEAD_DIM // 2, (dq_ref,))
    _rope_store(proj(C_DK, C_DV), tdk_ref, DIFF_HEAD_DIM // 2, dk_outs, dk_t)
    dv = proj(C_DV, C_GA)
    dvf_ref[...] = dv
    dvb_ref[...] = dv.astype(BF16)

    glu_ref[...] = proj(C_GA, C_GB) * jax.nn.sigmoid(proj(C_GB, C_END))


def _mix_in(x, mod, pre_g, w_in, q_g, w_uq, kv_g, w_uk, w_uv, tables, *, srows, layer, state=None, tm=256):
    rows = x.shape[0]
    prompt = srows > tm
    nseq, mod_spec = _mod_spec(tm, srows)
    ntab = tables[0].shape[1] // tm
    row = lambda n: pl.BlockSpec((tm, n), lambda i: (i, 0))
    tab_spec = pl.BlockSpec((3, tm, LANES), lambda i: (0, i % ntab, 0))
    out_cols = [(MLA_PAD, BF16), (MLA_KV_RANK, F32), (LANES, F32), (DIFF_W, BF16), (DIFF_W, BF16), (DIFF_W, BF16),
                (D_MODEL, F32)]
    out_cols += [(MLA_PAD, BF16), (MLA_PAD, BF16)] if prompt else [(DIFF_W, F32), (DIFF_W, F32)]
    out_specs = [row(n) for n, _ in out_cols]
    out_shape = [jax.ShapeDtypeStruct((rows, n), dt) for n, dt in out_cols]
    in_specs = [row(D_MODEL), mod_spec, _const_spec((1, D_MODEL)), _const_spec((D_MODEL, C_END)),
                _const_spec((1, MLA_Q_RANK)), _const_spec((MLA_Q_RANK, MLA_PAD)),
                _const_spec((1, MLA_KV_RANK)), _const_spec((MLA_KV_RANK, MLA_PAD)),
                _const_spec((MLA_KV_RANK, MLA_HEADS * MLA_V)),
                tab_spec, tab_spec, tab_spec, tab_spec]
    operands = [x, mod, pre_g, w_in, q_g, w_uq, kv_g, w_uk, w_uv, *tables]
    assert len(operands) == N_MIX_IN
    aliases = {}
    if prompt:
        per = srows // tm
        out_specs += [pl.BlockSpec((1, 1, DIFF_W, tm), lambda i: (layer, i // per, 0, i % per)),
                      pl.BlockSpec((1, tm, DIFF_W), lambda i: (layer, i, 0))]
        out_shape += [jax.ShapeDtypeStruct((DEPTH, rows // srows, DIFF_W, srows), F32),
                      jax.ShapeDtypeStruct((DEPTH, rows, DIFF_W), F32)]
        if state is not None:
            operands += list(state)
            in_specs += [pl.BlockSpec(memory_space=pl.ANY)] * len(state)
            aliases = {N_MIX_IN + k: len(out_shape) - len(state) + k for k in range(len(state))}
    return pl.pallas_call(
        functools.partial(_mix_in_kernel, nseq=nseq, prompt=prompt),
        grid=(rows // tm,),
        in_specs=in_specs,
        out_specs=out_specs,
        out_shape=out_shape,
        input_output_aliases=aliases,
        compiler_params=_cparams(("parallel",), 56),
        name="mix_in",
    )(*operands)


ATTN_TQ = 1024
ATTN_TK = 512


def _rep_lanes(x, n):
    return pltpu.repeat(x, n, axis=1) if n > 1 else x


def _online_step(s, m_ref, l_ref, rows):
    m_prev = m_ref[rows, :]
    m_new = jnp.maximum(m_prev, jnp.max(s, axis=-1, keepdims=True))
    alpha = jnp.exp2(m_prev - m_new)
    p = jnp.exp2(s - _rep_lanes(m_new, s.shape[1] // LANES))
    l_ref[rows, :] = alpha * l_ref[rows, :] + jnp.sum(p, axis=-1, keepdims=True)
    m_ref[rows, :] = m_new
    return p.astype(BF16), alpha


def _causal_sweep(i, tq, tk, tile_fn):
    per = tq // tk

    def body(j, carry):
        tile_fn(j, 0, None)
        return carry

    lax.fori_loop(0, i * per, body, 0)
    for dj in range(per):
        tile_fn(i * per + dj, dj * tk, dj)


def _diag_mask(s, row0, dj, tk):
    qc = (lax.broadcasted_iota(jnp.int32, s.shape, 0) + row0) // CHUNK
    kc = (lax.broadcasted_iota(jnp.int32, s.shape, 1) + dj * tk) // CHUNK
    return jnp.where(kc <= qc, s, -jnp.inf)


def _mla_attn_kernel(q_ref, k_ref, v_ref, o_ref, m_ref, l_ref, acc_ref, *, tq, tk):
    i = pl.program_id(2)
    m_ref[...] = jnp.full_like(m_ref, -jnp.inf)
    l_ref[...] = jnp.zeros_like(l_ref)
    acc_ref[...] = jnp.zeros_like(acc_ref)
    low = lax.broadcasted_iota(jnp.int32, (1, LANES), 1) < MLA_V

    def tile(j, row0, dj):
        keys = pl.ds(pl.multiple_of(j * tk, tk), tk)
        rows = pl.ds(row0, tq - row0)
        for hh in range(2):
            sl = slice(hh * HEAD_SLOT, (hh + 1) * HEAD_SLOT)
            s = _dot_nt(q_ref[rows, sl], k_ref[keys, sl])
            if dj is not None:
                s = _diag_mask(s, row0, dj, tk)
            p, alpha = _online_step(s, m_ref.at[hh], l_ref.at[hh], rows)
            mine = low if hh == 0 else jnp.logical_not(low)
            acc_ref[rows, :] = acc_ref[rows, :] * jnp.where(mine, alpha, 1.0) + _dot(p, v_ref[keys, sl])

    _causal_sweep(i, tq, tk, tile)
    o_ref[...] = (acc_ref[...] / jnp.where(low, l_ref[0], l_ref[1])).astype(o_ref.dtype)


def _mla_attn_prompt(q_pad, k_pad, v_pad, *, batch, seq, tq=ATTN_TQ, tk=ATTN_TK):
    nq = seq // tq
    pair = 2 * HEAD_SLOT
    stat = pltpu.VMEM((2, tq, LANES), F32)
    return pl.pallas_call(
        functools.partial(_mla_attn_kernel, tq=tq, tk=tk),
        grid=(batch, MLA_HEADS // 2, nq),
        in_specs=[pl.BlockSpec((tq, pair), lambda b, p, i: (b * nq + i, p)),
                  pl.BlockSpec((seq, pair), lambda b, p, i: (b, p)),
                  pl.BlockSpec((seq, pair), lambda b, p, i: (b, p))],
        out_specs=pl.BlockSpec((tq, LANES), lambda b, p, i: (b * nq + i, p)),
        out_shape=jax.ShapeDtypeStruct((batch * seq, MLA_HEADS * MLA_V), BF16),
        scratch_shapes=[stat, stat, pltpu.VMEM((tq, LANES), F32)],
        compiler_params=_cparams(("parallel", "parallel", "arbitrary"), 48),
        name="mla_attn",
    )(q_pad, k_pad, v_pad)


def _diff_lambda(lam_ref, lam_init):
    lp = lam_ref[...]
    s1 = jnp.sum(lp[0:1] * lp[1:2], axis=-1, keepdims=True)
    s2 = jnp.sum(lp[2:3] * lp[3:4], axis=-1, keepdims=True)
    return jnp.exp(s1) - jnp.exp(s2) + lam_init


def _diff_finish(o, g_ref, lam_init):
    return _rms(o) * g_ref[...] * (1.0 - lam_init)


def _split_maps(q):
    low = lax.broadcasted_iota(jnp.int32, (1, LANES), 1) < DIFF_HEAD_DIM
    zero = jnp.zeros_like(q)
    return jnp.where(low, q, zero), jnp.where(low, zero, q)


def _diff_attn_kernel(q_ref, k_ref, v_ref, lam_ref, g_ref, o_ref, qs_ref, m_ref, l_ref, acc_ref, *, tq, tk, lam_init):
    i = pl.program_id(2)
    m_ref[...] = jnp.full_like(m_ref, -jnp.inf)
    l_ref[...] = jnp.zeros_like(l_ref)
    acc_ref[...] = jnp.zeros_like(acc_ref)
    qs_ref[0], qs_ref[1] = _split_maps(q_ref[...])

    def tile(j, row0, dj):
        keys = pl.ds(pl.multiple_of(j * tk, tk), tk)
        rows = pl.ds(row0, tq - row0)
        k = k_ref[keys, :]
        v = v_ref[keys, :]
        for mp in range(2):
            s = _dot_nt(qs_ref[mp, rows, :], k)
            if dj is not None:
                s = _diag_mask(s, row0, dj, tk)
            p, alpha = _online_step(s, m_ref.at[mp], l_ref.at[mp], rows)
            acc_ref[mp, rows, :] = acc_ref[mp, rows, :] * alpha + _dot(p, v)

    _causal_sweep(i, tq, tk, tile)
    lam = _diff_lambda(lam_ref, lam_init)
    o = acc_ref[0] / l_ref[0] - lam * (acc_ref[1] / l_ref[1])
    o_ref[...] = _diff_finish(o, g_ref, lam_init).astype(o_ref.dtype)


def _diff_attn_prompt(dq, dk, dv, lam_p, sub_g, *, batch, seq, lam_init, tq=ATTN_TQ, tk=ATTN_TK):
    nq = seq // tq
    stat = pltpu.VMEM((2, tq, LANES), F32)
    return pl.pallas_call(
        functools.partial(_diff_attn_kernel, tq=tq, tk=tk, lam_init=lam_init),
        grid=(batch, DIFF_HEADS, nq),
        in_specs=[pl.BlockSpec((tq, LANES), lambda b, h, i: (b * nq + i, h)),
                  pl.BlockSpec((seq, LANES), lambda b, h, i: (b, h)),
                  pl.BlockSpec((seq, LANES), lambda b, h, i: (b, h)),
                  pl.BlockSpec((4, DIFF_HEAD_DIM), lambda b, h, i: (0, 0)),
                  pl.BlockSpec((1, DIFF_V_DIM), lambda b, h, i: (0, 0))],
        out_specs=pl.BlockSpec((tq, LANES), lambda b, h, i: (b * nq + i, h)),
        out_shape=jax.ShapeDtypeStruct((batch * seq, DIFF_W), BF16),
        scratch_shapes=[pltpu.VMEM((2, tq, LANES), BF16), stat, stat, stat],
        compiler_params=_cparams(("parallel", "parallel", "arbitrary"), 48),
        name="diff_attn",
    )(dq, dk, dv, lam_p, sub_g)


def _mla_sample_kernel(q_ref, ckvp_ref, kpet_ref, ckvn_ref, kpen_ref, wuk_ref, wuvp_ref, o_ref, *, group):
    ckv_p = ckvp_ref[0, 0].astype(BF16)
    past = ckv_p.shape[0]
    kpe_t = jnp.concatenate([kpet_ref[0, 0].astype(BF16), jnp.zeros((LANES - MLA_ROPE, past), BF16)], axis=0)
    ckv_n = ckvn_ref[...].astype(BF16)
    kpe_n = kpen_ref[...].astype(BF16)
    t = q_ref.shape[0]
    for g0 in range(0, MLA_HEADS, group):
        qlat, qfull, qpe = [], [], []
        for h in range(g0, g0 + group):
            sl = slice(h * HEAD_SLOT, (h + 1) * HEAD_SLOT)
            qh = q_ref[:, sl]
            qlat.append(_dot_nt(qh, wuk_ref[:, sl]).astype(BF16))
            qfull.append(qh)
            qpe.append(jnp.concatenate([qh[:, MLA_NOPE:MLA_NOPE + MLA_ROPE],
                                        jnp.zeros((t, LANES - MLA_ROPE), BF16)], axis=1))
        qlat = jnp.concatenate(qlat, axis=0)
        qfull = jnp.concatenate(qfull, axis=0)
        qpe = jnp.concatenate(qpe, axis=0)
        s_p = _dot_nt(qlat, ckv_p) + _dot(qpe, kpe_t)
        s_n = _dot_nt(qlat, ckv_n) + _dot_nt(qfull, kpe_n)
        m = jnp.maximum(jnp.max(s_p, axis=-1, keepdims=True), jnp.max(s_n, axis=-1, keepdims=True))
        p_p = jnp.exp2(s_p - m)
        p_n = jnp.exp2(s_n - m)
        l = jnp.sum(p_p, axis=-1, keepdims=True) + jnp.sum(p_n, axis=-1, keepdims=True)
        o_lat = ((_dot(p_p.astype(BF16), ckv_p) + _dot(p_n.astype(BF16), ckv_n)) / l).astype(BF16)
        for a in range(0, group, 2):
            h = g0 + a
            o_pair = (_dot(o_lat[a * t:(a + 1) * t], wuvp_ref[:, h * HEAD_SLOT:(h + 1) * HEAD_SLOT])
                      + _dot(o_lat[(a + 1) * t:(a + 2) * t], wuvp_ref[:, (h + 1) * HEAD_SLOT:(h + 2) * HEAD_SLOT]))
            o_ref[:, (h // 2) * LANES:(h // 2 + 1) * LANES] = o_pair.astype(o_ref.dtype)


def _mla_attn_sample(q_pad, ckv_cache, kpe_cache_t, ckv_new, kpe_new, w_uk, w_uv_pad, *, layer, batch, t):
    past = ckv_cache.shape[2]
    return pl.pallas_call(
        functools.partial(_mla_sample_kernel, group=4),
        grid=(batch,),
        in_specs=[pl.BlockSpec((t, MLA_PAD), lambda b: (b, 0)),
                  pl.BlockSpec((1, 1, past, MLA_KV_RANK), lambda b: (layer, b, 0, 0)),
                  pl.BlockSpec((1, 1, MLA_ROPE, past), lambda b: (layer, b, 0, 0)),
                  pl.BlockSpec((t, MLA_KV_RANK), lambda b: (b, 0)),
                  pl.BlockSpec((t, LANES), lambda b: (b, 0)),
                  _const_spec((MLA_KV_RANK, MLA_PAD)),
                  _const_spec((MLA_KV_RANK, MLA_PAD))],
        out_specs=pl.BlockSpec((t, MLA_HEADS * MLA_V), lambda b: (b, 0)),
        out_shape=jax.ShapeDtypeStruct((batch * t, MLA_HEADS * MLA_V), BF16),
        compiler_params=_cparams(("parallel",), 48),
        name="mla_attn_sample",
    )(q_pad, ckv_cache, kpe_cache_t, ckv_new, kpe_new, w_uk, w_uv_pad)


def _diff_sample_kernel(q_ref, kt_ref, vp_ref, kn_ref, vn_ref, lam_ref, g_ref, o_ref, *, lam_init):
    t = q_ref.shape[0]
    q0, q1 = _split_maps(q_ref[...])
    qm = jnp.concatenate([q0, q1], axis=0)
    s_p = _dot(qm, kt_ref[0, 0, 0].astype(BF16))
    s_n = _dot_nt(qm, kn_ref[...])
    m = jnp.maximum(jnp.max(s_p, axis=-1, keepdims=True), jnp.max(s_n, axis=-1, keepdims=True))
    p_p = jnp.exp2(s_p - m)
    p_n = jnp.exp2(s_n - m)
    inv = 1.0 / (jnp.sum(p_p, axis=-1, keepdims=True) + jnp.sum(p_n, axis=-1, keepdims=True))
    lam = _diff_lambda(lam_ref, lam_init)
    w0 = inv[:t]
    w1 = lam * inv[t:]
    a_p = (p_p[:t] * w0 - p_p[t:] * w1).astype(BF16)
    a_n = (p_n[:t] * w0 - p_n[t:] * w1).astype(BF16)
    o = _dot(a_p, vp_ref[0, 0].astype(BF16)) + _dot(a_n, vn_ref[...])
    o_ref[...] = _diff_finish(o, g_ref, lam_init).astype(o_ref.dtype)


def _diff_attn_sample(dq, k_cache_t, v_cache, dk_new, dv_new, lam_p, sub_g, *, layer, batch, t, lam_init):
    past = v_cache.shape[2]
    new = pl.BlockSpec((t, LANES), lambda b, h: (b, h))
    return pl.pallas_call(
        functools.partial(_diff_sample_kernel, lam_init=lam_init),
        grid=(batch, DIFF_HEADS),
        in_specs=[new,
                  pl.BlockSpec((1, 1, 1, LANES, past), lambda b, h: (layer, b, h, 0, 0)),
                  pl.BlockSpec((1, 1, past, LANES), lambda b, h: (layer, b, 0, h)),
                  new, new,
                  pl.BlockSpec((4, DIFF_HEAD_DIM), lambda b, h: (0, 0)),
                  pl.BlockSpec((1, DIFF_V_DIM), lambda b, h: (0, 0))],
        out_specs=new,
        out_shape=jax.ShapeDtypeStruct((batch * t, DIFF_W), BF16),
        compiler_params=_cparams(("parallel", "parallel"), 40),
        name="diff_attn_sample",
    )(dq, k_cache_t, v_cache, dk_new, dv_new, lam_p, sub_g)


def _conv_kernel(cur_ref, prev_ref, wdw_ref, bdw_ref, lng_ref, lnb_ref, o_ref, xin_ref, y_ref, *, rc, nr):
    tm = cur_ref.shape[0]
    xin_ref[0:CONV_HALO, :] = prev_ref[0]
    xin_ref[CONV_HALO:CONV_HALO + tm, :] = cur_ref[...]
    xin_ref[CONV_HALO + tm:, :] = jnp.zeros((SUBLANES, D_MODEL), F32)
    first = CONV_HALO - CONV_STATE

    def taps(r, carry):
        r0 = pl.multiple_of(r * rc, rc)
        for c in range(D_MODEL // LANES):
            cl = slice(c * LANES, (c + 1) * LANES)
            y = jnp.zeros((rc, LANES), F32) + bdw_ref[:, cl]
            for s in range(SUBLANES):
                z = None
                for w in range(CONV_WIDTH):
                    if (first + w) % SUBLANES == s:
                        start = pl.multiple_of(r0 + (first + w - s), SUBLANES)
                        term = xin_ref[pl.ds(start, rc + SUBLANES), cl] * wdw_ref[w:w + 1, cl]
                        z = term if z is None else z + term
                y = y + z[s:s + rc]
            y_ref[pl.ds(r0, rc), cl] = y
        return carry

    lax.fori_loop(0, tm // rc, taps, 0)

    def norm(r, carry):
        r0 = pl.multiple_of(r * nr, nr)
        y = y_ref[pl.ds(r0, nr), :]
        mu = jnp.mean(y, axis=-1, keepdims=True)
        yc = y - mu
        var = jnp.mean(yc * yc, axis=-1, keepdims=True)
        yn = yc * lax.rsqrt(var + EPS) * lng_ref[...] + lnb_ref[...]
        o_ref[pl.ds(r0, nr), :] = (yn * jax.nn.sigmoid(yn)).astype(o_ref.dtype)
        return carry

    lax.fori_loop(0, tm // nr, norm, 0)


def _conv(glu, prev, w_dw, b_dw, ln_g, ln_b, *, tm):
    rows = glu.shape[0]
    vec = pl.BlockSpec((1, D_MODEL), lambda i: (0, 0))
    return pl.pallas_call(
        functools.partial(_conv_kernel, rc=64, nr=min(128, tm)),
        grid=(rows // tm,),
        in_specs=[pl.BlockSpec((tm, D_MODEL), lambda i: (i, 0)),
                  pl.BlockSpec((1, CONV_HALO, D_MODEL), lambda i: (i, 0, 0)),
                  pl.BlockSpec((CONV_WIDTH, D_MODEL), lambda i: (0, 0)),
                  vec, vec, vec],
        out_specs=pl.BlockSpec((tm, D_MODEL), lambda i: (i, 0)),
        out_shape=jax.ShapeDtypeStruct((rows, D_MODEL), BF16),
        scratch_shapes=[pltpu.VMEM((tm + CONV_HALO + SUBLANES, D_MODEL), F32), pltpu.VMEM((tm, D_MODEL), F32)],
        compiler_params=_cparams(("parallel",), 24),
        name="conv",
    )(glu, prev, w_dw, b_dw, ln_g, ln_b)


def _merge_kernel(x_ref, mod_ref, preg_ref, postg_ref, am_ref, ad_ref, yc_ref,
                  wbg_ref, bbg_ref, wmo_ref, wdo_ref, wpw_ref, bpw_ref, wout_ref, o_ref, *, nseq):
    x = x_ref[...]
    mod = mod_ref[...]
    u = _mod_norm(x, preg_ref[...], mod, nseq).astype(BF16)
    branches = (_dot(am_ref[...], wmo_ref[...]),
                _dot(ad_ref[...], wdo_ref[...]),
                _dot(yc_ref[...], wpw_ref[...]) + bpw_ref[...])
    merged = None
    for k, br in enumerate(branches):
        sl = slice(k * D_MODEL, (k + 1) * D_MODEL)
        gate = jax.nn.sigmoid(_dot(u, wbg_ref[:, sl]) + bbg_ref[:, sl])
        merged = gate * br if merged is None else merged + gate * br
    o = _dot(merged.astype(BF16), wout_ref[...])
    o_ref[...] = _gated_residual(x, o, postg_ref[...], mod, nseq, 1.0)


def _merge(x, mod, pre_g, post_g, a_mla, a_diff, y_conv, w_bg, b_bg, w_mo, w_do, w_pw, b_pw, w_out, *, srows, tm=512):
    rows = x.shape[0]
    nseq, mod_spec = _mod_spec(tm, srows)
    row = pl.BlockSpec((tm, D_MODEL), lambda i: (i, 0))
    sq = _const_spec((D_MODEL, D_MODEL))
    vec = _const_spec((1, D_MODEL))
    return pl.pallas_call(
        functools.partial(_merge_kernel, nseq=nseq),
        grid=(rows // tm,),
        in_specs=[row, mod_spec, vec, vec, row, row, row,
                  _const_spec((D_MODEL, 3 * D_MODEL)), _const_spec((1, 3 * D_MODEL)), sq, sq, sq, vec, sq],
        out_specs=row,
        out_shape=jax.ShapeDtypeStruct((rows, D_MODEL), F32),
        compiler_params=_cparams(("parallel",), 56),
        name="merge",
    )(x, mod, pre_g, post_g, a_mla, a_diff, y_conv, w_bg, b_bg, w_mo, w_do, w_pw, b_pw, w_out)


def _rope_angles(pos, rot_dim):
    half = rot_dim // 2
    inv_freq = ROPE_THETA ** (-jnp.arange(half, dtype=F32) / half)
    ang = pos.astype(F32)[:, None] * inv_freq[None, :]
    return jnp.cos(ang), jnp.sin(ang)


def _mla_tables(pos, scale):
    cos, sin = _rope_angles(pos, MLA_ROPE)
    t = pos.shape[0]
    z = lambda n: jnp.zeros((t, n), F32)
    pad = HEAD_SLOT - MLA_NOPE - MLA_ROPE
    c = jnp.concatenate([jnp.ones((t, MLA_NOPE), F32), cos, cos, z(pad)], axis=1)
    s1 = jnp.concatenate([z(MLA_NOPE), -sin, z(MLA_ROPE // 2), z(pad)], axis=1)
    s2 = jnp.concatenate([z(MLA_NOPE), z(MLA_ROPE // 2), sin, z(pad)], axis=1)
    return jnp.stack([c, s1, s2]) * scale


def _diff_tables(pos, scale):
    cos, sin = _rope_angles(pos, DIFF_HEAD_DIM)
    z = jnp.zeros_like(sin)
    c = jnp.concatenate([cos, cos, cos, cos], axis=1)
    s1 = jnp.concatenate([-sin, z, -sin, z], axis=1)
    s2 = jnp.concatenate([z, sin, z, sin], axis=1)
    return jnp.stack([c, s1, s2]) * scale


def _tables(pos, reps):
    tabs = (_mla_tables(pos, MLA_SCALE * LOG2E), _mla_tables(pos, 1.0),
            _diff_tables(pos, DIFF_SCALE * LOG2E), _diff_tables(pos, 1.0))
    return tuple(jnp.tile(t, (1, reps, 1)) for t in tabs)


def _pad_heads(w, heads, width):
    k = w.shape[0]
    w = w.reshape(k, heads, width)
    return jnp.pad(w, ((0, 0), (0, 0), (0, HEAD_SLOT - width))).reshape(k, heads * HEAD_SLOT)


def _pad_pairs(w):
    k = w.shape[0]
    w = w.reshape(k, MLA_HEADS // 2, 2, MLA_V)
    z = jnp.zeros_like(w[:, :, 0])
    lo = jnp.concatenate([w[:, :, 0], z], axis=-1)
    hi = jnp.concatenate([z, w[:, :, 1]], axis=-1)
    return jnp.stack([lo, hi], axis=2).reshape(k, MLA_PAD)


def _permute_w_in(w):
    s0 = MLA_Q_RANK + MLA_KV_RANK
    z = lambda n: jnp.zeros((w.shape[0], n), w.dtype)
    return jnp.concatenate([w[:, :s0], z(MLA_NOPE), w[:, s0:s0 + MLA_ROPE], z(HEAD_SLOT - MLA_NOPE - MLA_ROPE),
                            w[:, s0 + MLA_ROPE:]], axis=1)


def _conv_prev_prompt(glu, batch, seq, tm):
    g = glu.reshape(batch, seq // tm, tm, D_MODEL)[:, :-1, tm - CONV_HALO:, :]
    g = jnp.pad(g, ((0, 0), (1, 0), (0, 0), (0, 0)))
    return g.reshape(batch * (seq // tm), CONV_HALO, D_MODEL)


def kernel(x_prompt, x_sample, c_prompt, c_sample, cache_mla_ckv, cache_mla_kpe, cache_diff_k, cache_diff_v, state_conv, ada_w, ada_b, ffn1_pre_g, ffn1_post_g, ffn1_w_gate, ffn1_w_up, ffn1_w_down, mix_pre_g, mix_post_g, w_in, mla_q_norm_g, mla_w_uq, mla_kv_norm_g, mla_w_uk, mla_w_uv, mla_w_o, diff_lq1, diff_lk1, diff_lq2, diff_lk2, diff_subln_g, diff_w_o, conv_w_dw, conv_b_dw, conv_ln_g, conv_ln_b, conv_w_pw2, conv_b_pw2, w_branch_gate, b_branch_gate, w_out, ffn2_pre_g, ffn2_post_g, ffn2_w_gate, ffn2_w_up, ffn2_w_down):
    pb, pt, _ = x_prompt.shape
    sb, st, _ = x_sample.shape
    past = cache_mla_ckv.shape[2]
    assert st == CHUNK and past % CHUNK == 0, "sample rows must be exactly the chunk that follows the cache"

    mods = _ada(jnp.concatenate([c_prompt, c_sample], axis=0), ada_w, ada_b)
    mods = mods.reshape(DEPTH, pb + sb, 3, 3, D_MODEL)

    mix_tm = 256
    tabs_p = _tables(jnp.arange(pt, dtype=jnp.int32), 1)
    tabs_s = _tables(past + jnp.arange(st, dtype=jnp.int32), mix_tm // st)

    kpe_cache_t = jnp.swapaxes(cache_mla_kpe, 2, 3)
    dk_cache_t = jnp.transpose(cache_diff_k, (0, 1, 3, 4, 5, 2)).reshape(DEPTH, sb, DIFF_HEADS, DIFF_V_DIM, past)
    dv_cache = cache_diff_v.reshape(DEPTH, sb, past, DIFF_W)

    xs = {"p": x_prompt.reshape(pb * pt, D_MODEL), "s": x_sample.reshape(sb * st, D_MODEL)}
    srows = {"p": pt, "s": st}
    states = {"p": [], "s": []}
    diff_state = None
    bf = lambda a: a.astype(BF16)
    vec = lambda a: a.reshape(1, -1)

    for l in range(DEPTH):
        lam_init = 0.8 - 0.6 * math.exp(-0.3 * l)
        w = dict(
            f1=(vec(ffn1_pre_g[l]), vec(ffn1_post_g[l]), bf(ffn1_w_gate[l]), bf(ffn1_w_up[l]), bf(ffn1_w_down[l])),
            f2=(vec(ffn2_pre_g[l]), vec(ffn2_post_g[l]), bf(ffn2_w_gate[l]), bf(ffn2_w_up[l]), bf(ffn2_w_down[l])),
            w_in=bf(_permute_w_in(w_in[l])),
            w_uq=bf(_pad_heads(mla_w_uq[l], MLA_HEADS, MLA_NOPE + MLA_ROPE)),
            w_uk=bf(_pad_heads(mla_w_uk[l], MLA_HEADS, MLA_NOPE)),
            w_uv=bf(mla_w_uv[l]),
            w_uv_pad=bf(_pad_pairs(mla_w_uv[l])),
            lam=jnp.stack([diff_lq1[l], diff_lk1[l], diff_lq2[l], diff_lk2[l]]),
        )
        for key in ("p", "s"):
            x = xs[key]
            sr = srows[key]
            mod = mods[l, :pb] if key == "p" else mods[l, pb:]
            x = _ffn(x, mod[:, 0], *w["f1"], srows=sr)
            outs = _mix_in(x, mod[:, 1], vec(mix_pre_g[l]), w["w_in"], vec(mla_q_norm_g[l]), w["w_uq"],
                           vec(mla_kv_norm_g[l]), w["w_uk"], w["w_uv"], tabs_p if key == "p" else tabs_s,
                           srows=sr, layer=l, state=diff_state if key == "p" else None, tm=mix_tm)
            q_pad, ckv, kpe, dq, dk_b, dv_b, glu = outs[:7]
            if key == "p":
                k_pad, v_pad, *diff_state = outs[7:]
                dk_f = dv_f = None
                a_mla = _mla_attn_prompt(q_pad, k_pad, v_pad, batch=pb, seq=pt)
                a_diff = _diff_attn_prompt(dq, dk_b, dv_b, w["lam"], vec(diff_subln_g[l]),
                                           batch=pb, seq=pt, lam_init=lam_init)
                conv_tm = 256
                prev = _conv_prev_prompt(glu, pb, pt, conv_tm)
                new_conv = glu.reshape(pb, pt, D_MODEL)[:, pt - CONV_STATE:, :]
                nb, nt = pb, pt
            else:
                dk_f, dv_f = outs[7:]
                a_mla = _mla_attn_sample(q_pad, cache_mla_ckv, kpe_cache_t, ckv, kpe, w["w_uk"],
                                         w["w_uv_pad"], layer=l, batch=sb, t=st)
                a_diff = _diff_attn_sample(dq, dk_cache_t, dv_cache, dk_b, dv_b, w["lam"],
                                           vec(diff_subln_g[l]), layer=l, batch=sb, t=st, lam_init=lam_init)
                conv_tm = st
                prev = jnp.pad(state_conv[l], ((0, 0), (CONV_HALO - CONV_STATE, 0), (0, 0)))
                xin = jnp.concatenate([state_conv[l], glu.reshape(sb, st, D_MODEL)], axis=1)
                new_conv = xin[:, st:, :]
                nb, nt = sb, st
            y_conv = _conv(glu, prev, conv_w_dw[l], vec(conv_b_dw[l]), vec(conv_ln_g[l]), vec(conv_ln_b[l]),
                           tm=conv_tm)
            x = _merge(x, mod[:, 1], vec(mix_pre_g[l]), vec(mix_post_g[l]), a_mla, a_diff, y_conv,
                       bf(w_branch_gate[l]), vec(b_branch_gate[l]), bf(mla_w_o[l]), bf(diff_w_o[l]),
                       bf(conv_w_pw2[l]), vec(conv_b_pw2[l]), bf(w_out[l]), srows=sr)
            x = _ffn(x, mod[:, 2], *w["f2"], srows=sr)
            xs[key] = x
            states[key].append((
                ckv.reshape(nb, nt, MLA_KV_RANK),
                kpe[:, MLA_NOPE:MLA_NOPE + MLA_ROPE].reshape(nb, nt, MLA_ROPE),
                None if dk_f is None else dk_f.reshape(nb, nt, DIFF_HEADS, 2, DIFF_HEAD_DIM),
                None if dv_f is None else dv_f.reshape(nb, nt, DIFF_HEADS, DIFF_V_DIM),
                new_conv))

    stack = lambda key, k: jnp.stack([s[k] for s in states[key]])
    dk_t, dv_s = diff_state
    dk_prompt = jnp.transpose(dk_t.reshape(DEPTH, pb, DIFF_HEADS, 2, DIFF_HEAD_DIM, pt), (0, 1, 5, 2, 3, 4))
    dv_prompt = dv_s.reshape(DEPTH, pb, pt, DIFF_HEADS, DIFF_V_DIM)
    return (xs["p"].reshape(pb, pt, D_MODEL), xs["s"].reshape(sb, st, D_MODEL),
            stack("p", 0), stack("p", 1), dk_prompt, dv_prompt, stack("p", 4),
            *(stack("s", k) for k in range(5)))
```

```python
import functools
import math

import jax
import jax.numpy as jnp
from jax import lax
from jax.experimental import pallas as pl
from jax.experimental.pallas import tpu as pltpu

D_MODEL = 1024
DEPTH = 2
CHUNK = 64
ROPE_THETA = 10000.0
EPS = 1e-6
MACARON_W = 0.5

MLA_HEADS = 16
MLA_NOPE = 64
MLA_ROPE = 32
MLA_V = 64
MLA_KV_RANK = 256
MLA_Q_RANK = 512
MLA_SCALE = (MLA_NOPE + MLA_ROPE) ** -0.5

DIFF_HEADS = 8
DIFF_HEAD_DIM = 64
DIFF_V_DIM = 2 * DIFF_HEAD_DIM
DIFF_SCALE = DIFF_HEAD_DIM ** -0.5
LOG2E = math.log2(math.e)

CONV_WIDTH = 31
CONV_STATE = CONV_WIDTH - 1
CONV_HALO = 32
D_FF = 2816
N_ADA = 9

LANES = 128
SUBLANES = 8
HEAD_SLOT = LANES
MLA_PAD = MLA_HEADS * HEAD_SLOT
MLA_ONE_LANE = (MLA_V, 0)
DIFF_W = DIFF_HEADS * 2 * DIFF_HEAD_DIM
MIB = 1024 * 1024

C_CQ = 0
C_CKV = C_CQ + MLA_Q_RANK
C_KPE = C_CKV + MLA_KV_RANK
C_DQ = C_KPE + HEAD_SLOT
C_DK = C_DQ + DIFF_W
C_DV = C_DK + DIFF_W
C_GA = C_DV + DIFF_W
C_GB = C_GA + D_MODEL
C_END = C_GB + D_MODEL

F32 = jnp.float32
BF16 = jnp.bfloat16


def _cparams(sem, vmem_mib):
    return pltpu.CompilerParams(dimension_semantics=sem, vmem_limit_bytes=int(vmem_mib * MIB))


def _dot(a, b):
    return jnp.dot(a, b, preferred_element_type=F32)


def _dot_nt(a, b):
    return lax.dot_general(a, b, (((1,), (1,)), ((), ())), preferred_element_type=F32)


def _rms(x):
    return x * lax.rsqrt(jnp.mean(x * x, axis=-1, keepdims=True) + EPS)


def _mod_norm(x, g, mod, nseq):
    tm = x.shape[0]
    xn = _rms(x) * g
    sh = mod[:, 0:1, :]
    sc = mod[:, 1:2, :]
    if nseq == 1:
        return xn * (1.0 + sc[0]) + sh[0]
    xn = xn.reshape(nseq, tm // nseq, x.shape[1])
    return (xn * (1.0 + sc) + sh).reshape(tm, x.shape[1])


def _gated_residual(x, y, post_g, mod, nseq, weight):
    tm = x.shape[0]
    yn = _rms(y) * post_g
    gate = mod[:, 2:3, :]
    if nseq == 1:
        return x + weight * gate[0] * yn
    yn = yn.reshape(nseq, tm // nseq, x.shape[1])
    return x + (weight * gate * yn).reshape(tm, x.shape[1])


def _mod_spec(tm, srows):
    if tm <= srows:
        per = srows // tm
        return 1, pl.BlockSpec((1, 3, D_MODEL), lambda *idx: (idx[0] // per, 0, 0))
    nseq = tm // srows
    return nseq, pl.BlockSpec((nseq, 3, D_MODEL), lambda *idx: (idx[0], 0, 0))


def _const_spec(shape):
    nd = len(shape)
    return pl.BlockSpec(shape, lambda *idx: (0,) * nd, pipeline_mode=pl.Buffered(1))


def _ada_kernel(c_ref, w_ref, b_ref, o_ref):
    c = c_ref[...]
    a = (c * jax.nn.sigmoid(c)).astype(BF16)
    o_ref[0] = _dot(a, w_ref[0].astype(BF16)) + b_ref[0]


def _ada(c_all, ada_w, ada_b):
    nb = c_all.shape[0]
    tn = 1024
    n = N_ADA * D_MODEL
    return pl.pallas_call(
        _ada_kernel,
        grid=(DEPTH, n // tn),
        in_specs=[pl.BlockSpec((nb, D_MODEL), lambda l, j: (0, 0)),
                  pl.BlockSpec((1, D_MODEL, tn), lambda l, j: (l, 0, j)),
                  pl.BlockSpec((1, 1, tn), lambda l, j: (l, 0, j))],
        out_specs=pl.BlockSpec((1, nb, tn), lambda l, j: (l, 0, j)),
        out_shape=jax.ShapeDtypeStruct((DEPTH, nb, n), F32),
        compiler_params=_cparams(("parallel", "parallel"), 24),
        name="ada",
    )(c_all, ada_w, ada_b.reshape(DEPTH, 1, n))


def _ffn_kernel(x_ref, mod_ref, preg_ref, postg_ref, wg_ref, wu_ref, wd_ref, o_ref, *, nseq):
    x = x_ref[...]
    mod = mod_ref[...]
    h = _mod_norm(x, preg_ref[...], mod, nseq).astype(BF16)
    g = _dot(h, wg_ref[...])
    u = _dot(h, wu_ref[...])
    a = (g * jax.nn.sigmoid(g) * u).astype(BF16)
    o_ref[...] = _gated_residual(x, _dot(a, wd_ref[...]), postg_ref[...], mod, nseq, MACARON_W)


def _ffn(x, mod, pre_g, post_g, wg, wu, wd, *, srows, tm=512):
    rows = x.shape[0]
    nseq, mod_spec = _mod_spec(tm, srows)
    row_spec = pl.BlockSpec((tm, D_MODEL), lambda i: (i, 0))
    vec_spec = _const_spec((1, D_MODEL))
    return pl.pallas_call(
        functools.partial(_ffn_kernel, nseq=nseq),
        grid=(rows // tm,),
        in_specs=[row_spec, mod_spec, vec_spec, vec_spec,
                  _const_spec((D_MODEL, D_FF)), _const_spec((D_MODEL, D_FF)), _const_spec((D_FF, D_MODEL))],
        out_specs=row_spec,
        out_shape=jax.ShapeDtypeStruct((rows, D_MODEL), F32),
        compiler_params=_cparams(("parallel",), 56),
        name="ffn",
    )(x, mod, pre_g, post_g, wg, wu, wd)


def _rope_store(x, tbl_ref, shift, out_refs, t_ref=None):
    n = x.shape[1]
    c, s1, s2 = tbl_ref[0], tbl_ref[1], tbl_ref[2]
    xm = pltpu.roll(x, n - shift, 1)
    xp = pltpu.roll(x, shift, 1)
    for b in range(n // LANES):
        sl = slice(b * LANES, (b + 1) * LANES)
        r = x[:, sl] * c + xm[:, sl] * s1 + xp[:, sl] * s2
        for ref in out_refs:
            ref[:, sl] = r.astype(ref.dtype)
        if t_ref is not None:
            t_ref[sl, :] = r.T


N_MIX_IN = 13


def _mix_in_kernel(*refs, nseq, prompt):
    (x_ref, mod_ref, preg_ref, win_ref, qg_ref, wuq_ref, kvg_ref, wuk_ref, wuv_ref,
     tmq_ref, tmk_ref, tdq_ref, tdk_ref) = refs[:N_MIX_IN]
    outs = refs[len(refs) - (11 if prompt else 9):]
    qpad_ref, ckv_ref, kpe_ref, dq_ref, dkb_ref, dvb_ref, glu_ref = outs[:7]
    if prompt:
        kpad_ref, vpad_ref, dkt_ref, dvs_ref = outs[7:]
        dk_outs, dk_t, dvf_ref = (dkb_ref,), dkt_ref.at[0, 0], dvs_ref.at[0]
    else:
        dkf_ref, dvf_ref = outs[7:]
        dk_outs, dk_t = (dkf_ref, dkb_ref), None
    u = _mod_norm(x_ref[...], preg_ref[...], mod_ref[...], nseq).astype(BF16)

    def proj(a, b):
        return _dot(u, win_ref[:, a:b])

    cqn = (_rms(proj(C_CQ, C_CKV)) * qg_ref[...]).astype(BF16)
    _rope_store(_dot(cqn, wuq_ref[...]), tmq_ref, MLA_ROPE // 2, (qpad_ref,))

    ckvn = _rms(proj(C_CKV, C_KPE)) * kvg_ref[...]
    ckv_ref[...] = ckvn
    _rope_store(proj(C_KPE, C_DQ), tmk_ref, MLA_ROPE // 2, (kpe_ref,))

    if prompt:
        ckvb = ckvn.astype(BF16)
        kpe = kpe_ref[...]
        kn = _dot(ckvb, wuk_ref[...])
        for h in range(MLA_HEADS):
            sl = slice(h * HEAD_SLOT, (h + 1) * HEAD_SLOT)
            kpad_ref[:, sl] = (kn[:, sl] + kpe).astype(BF16)
        v = _dot(ckvb, wuv_ref[...])
        lane = lax.broadcasted_iota(jnp.int32, (1, LANES), 1)
        low = lane < MLA_V
        one_even = (lane == MLA_ONE_LANE[0]).astype(F32)
        one_odd = (lane == MLA_ONE_LANE[1]).astype(F32)
        for p in range(MLA_HEADS // 2):
            vp = v[:, p * LANES:(p + 1) * LANES]
            vpad_ref[:, (2 * p) * LANES:(2 * p + 1) * LANES] = jnp.where(low, vp, one_even).astype(BF16)
            vpad_ref[:, (2 * p + 1) * LANES:(2 * p + 2) * LANES] = jnp.where(low, one_odd, vp).astype(BF16)

    _rope_store(proj(C_DQ, C_DK), tdq_ref, DIFF_HEAD_DIM // 2, (dq_ref,))
    _rope_store(proj(C_DK, C_DV), tdk_ref, DIFF_HEAD_DIM // 2, dk_outs, dk_t)
    dv = proj(C_DV, C_GA)
    dvf_ref[...] = dv
    dvb_ref[...] = dv.astype(BF16)

    glu_ref[...] = proj(C_GA, C_GB) * jax.nn.sigmoid(proj(C_GB, C_END))


def _mix_in(x, mod, pre_g, w_in, q_g, w_uq, kv_g, w_uk, w_uv, tables, *, srows, layer, state=None, tm=256):
    rows = x.shape[0]
    prompt = srows > tm
    nseq, mod_spec = _mod_spec(tm, srows)
    ntab = tables[0].shape[1] // tm
    row = lambda n: pl.BlockSpec((tm, n), lambda i: (i, 0))
    tab_spec = pl.BlockSpec((3, tm, LANES), lambda i: (0, i % ntab, 0))
    out_cols = [(MLA_PAD, BF16), (MLA_KV_RANK, F32), (LANES, F32), (DIFF_W, BF16), (DIFF_W, BF16), (DIFF_W, BF16),
                (D_MODEL, F32)]
    out_cols += [(MLA_PAD, BF16), (MLA_PAD, BF16)] if prompt else [(DIFF_W, F32), (DIFF_W, F32)]
    out_specs = [row(n) for n, _ in out_cols]
    out_shape = [jax.ShapeDtypeStruct((rows, n), dt) for n, dt in out_cols]
    in_specs = [row(D_MODEL), mod_spec, _const_spec((1, D_MODEL)), _const_spec((D_MODEL, C_END)),
                _const_spec((1, MLA_Q_RANK)), _const_spec((MLA_Q_RANK, MLA_PAD)),
                _const_spec((1, MLA_KV_RANK)), _const_spec((MLA_KV_RANK, MLA_PAD)),
                _const_spec((MLA_KV_RANK, MLA_HEADS * MLA_V)),
                tab_spec, tab_spec, tab_spec, tab_spec]
    operands = [x, mod, pre_g, w_in, q_g, w_uq, kv_g, w_uk, w_uv, *tables]
    assert len(operands) == N_MIX_IN
    aliases = {}
    if prompt:
        per = srows // tm
        out_specs += [pl.BlockSpec((1, 1, DIFF_W, tm), lambda i: (layer, i // per, 0, i % per)),
                      pl.BlockSpec((1, tm, DIFF_W), lambda i: (layer, i, 0))]
        out_shape += [jax.ShapeDtypeStruct((DEPTH, rows // srows, DIFF_W, srows), F32),
                      jax.ShapeDtypeStruct((DEPTH, rows, DIFF_W), F32)]
        if state is not None:
            operands += list(state)
            in_specs += [pl.BlockSpec(memory_space=pl.ANY)] * len(state)
            aliases = {N_MIX_IN + k: len(out_shape) - len(state) + k for k in range(len(state))}
    return pl.pallas_call(
        functools.partial(_mix_in_kernel, nseq=nseq, prompt=prompt),
        grid=(rows // tm,),
        in_specs=in_specs,
        out_specs=out_specs,
        out_shape=out_shape,
        input_output_aliases=aliases,
        compiler_params=_cparams(("parallel",), 56),
        name="mix_in",
    )(*operands)


ATTN_TQ = 1024
ATTN_TK = 512


def _rep_lanes(x, n):
    return jnp.concatenate([x] * n, axis=1) if n > 1 else x


def _online_step(s, m_ref, l_ref, rows):
    m_prev = m_ref[rows, :]
    m_new = jnp.maximum(m_prev, jnp.max(s, axis=-1, keepdims=True))
    alpha = jnp.exp2(m_prev - m_new)
    p = jnp.exp2((s - _rep_lanes(m_new, s.shape[1] // LANES)).astype(BF16))
    l_ref[rows, :] = alpha * l_ref[rows, :] + jnp.sum(p.astype(F32), axis=-1, keepdims=True)
    m_ref[rows, :] = m_new
    return p, alpha


def _causal_sweep(i, tq, tk, tile_fn):
    per = tq // tk

    def body(j, carry):
        tile_fn(j, 0, None)
        return carry

    lax.fori_loop(0, i * per, body, 0)
    for dj in range(per):
        tile_fn(i * per + dj, dj * tk, dj)


def _diag_mask(s, row0, dj, tk):
    qc = (lax.broadcasted_iota(jnp.int32, s.shape, 0) + row0) // CHUNK
    kc = (lax.broadcasted_iota(jnp.int32, s.shape, 1) + dj * tk) // CHUNK
    return jnp.where(kc <= qc, s, -jnp.inf)


def _mla_attn_kernel(q_ref, k_ref, v_ref, o_ref, m_ref, acc_ref, *, tq, tk):
    i = pl.program_id(2)
    m_ref[...] = jnp.full_like(m_ref, -jnp.inf)
    acc_ref[...] = jnp.zeros_like(acc_ref)

    def tile(j, row0, dj):
        keys = pl.ds(pl.multiple_of(j * tk, tk), tk)
        rows = pl.ds(row0, tq - row0)
        for hh in range(2):
            sl = slice(hh * HEAD_SLOT, (hh + 1) * HEAD_SLOT)
            s = _dot_nt(q_ref[rows, sl], k_ref[keys, sl])
            if dj is not None:
                s = _diag_mask(s, row0, dj, tk)
            m_prev = m_ref[hh, rows, :]
            m_new = jnp.maximum(m_prev, jnp.max(s, axis=-1, keepdims=True))
            p = jnp.exp2((s - _rep_lanes(m_new, tk // LANES)).astype(BF16))
            m_ref[hh, rows, :] = m_new
            acc_ref[hh, rows, :] = acc_ref[hh, rows, :] * jnp.exp2(m_prev - m_new) + _dot(p, v_ref[keys, sl])

    _causal_sweep(i, tq, tk, tile)
    low = lax.broadcasted_iota(jnp.int32, (1, LANES), 1) < MLA_V
    a0, a1 = acc_ref[0], acc_ref[1]
    o_ref[...] = jnp.where(low, a0 / a0[:, MLA_ONE_LANE[0]:MLA_ONE_LANE[0] + 1],
                           a1 / a1[:, MLA_ONE_LANE[1]:MLA_ONE_LANE[1] + 1]).astype(o_ref.dtype)


def _mla_attn_prompt(q_pad, k_pad, v_pad, *, batch, seq, tq=ATTN_TQ, tk=ATTN_TK):
    nq = seq // tq
    pair = 2 * HEAD_SLOT
    stat = pltpu.VMEM((2, tq, LANES), F32)
    return pl.pallas_call(
        functools.partial(_mla_attn_kernel, tq=tq, tk=tk),
        grid=(batch, MLA_HEADS // 2, nq),
        in_specs=[pl.BlockSpec((tq, pair), lambda b, p, i: (b * nq + i, p)),
                  pl.BlockSpec((seq, pair), lambda b, p, i: (b, p)),
                  pl.BlockSpec((seq, pair), lambda b, p, i: (b, p))],
        out_specs=pl.BlockSpec((tq, LANES), lambda b, p, i: (b * nq + i, p)),
        out_shape=jax.ShapeDtypeStruct((batch * seq, MLA_HEADS * MLA_V), BF16),
        scratch_shapes=[stat, stat],
        compiler_params=_cparams(("parallel", "parallel", "arbitrary"), 48),
        name="mla_attn",
    )(q_pad, k_pad, v_pad)


def _diff_lambda(lam_ref, lam_init):
    lp = lam_ref[...]
    s1 = jnp.sum(lp[0:1] * lp[1:2], axis=-1, keepdims=True)
    s2 = jnp.sum(lp[2:3] * lp[3:4], axis=-1, keepdims=True)
    return jnp.exp(s1) - jnp.exp(s2) + lam_init


def _diff_finish(o, g_ref, lam_init):
    return _rms(o) * g_ref[...] * (1.0 - lam_init)


def _split_maps(q):
    low = lax.broadcasted_iota(jnp.int32, (1, LANES), 1) < DIFF_HEAD_DIM
    zero = jnp.zeros_like(q)
    return jnp.where(low, q, zero), jnp.where(low, zero, q)


def _diff_attn_kernel(q_ref, k_ref, v_ref, lam_ref, g_ref, o_ref, qs_ref, m_ref, l_ref, acc_ref, *, tq, tk, lam_init):
    i = pl.program_id(2)
    m_ref[...] = jnp.full_like(m_ref, -jnp.inf)
    l_ref[...] = jnp.zeros_like(l_ref)
    acc_ref[...] = jnp.zeros_like(acc_ref)
    qs_ref[0], qs_ref[1] = _split_maps(q_ref[...])

    def tile(j, row0, dj):
        keys = pl.ds(pl.multiple_of(j * tk, tk), tk)
        rows = pl.ds(row0, tq - row0)
        k = k_ref[keys, :]
        v = v_ref[keys, :]
        for mp in range(2):
            s = _dot_nt(qs_ref[mp, rows, :], k)
            if dj is not None:
                s = _diag_mask(s, row0, dj, tk)
            p, alpha = _online_step(s, m_ref.at[mp], l_ref.at[mp], rows)
            acc_ref[mp, rows, :] = acc_ref[mp, rows, :] * alpha + _dot(p, v)

    _causal_sweep(i, tq, tk, tile)
    lam = _diff_lambda(lam_ref, lam_init)
    o = acc_ref[0] / l_ref[0] - lam * (acc_ref[1] / l_ref[1])
    o_ref[...] = _diff_finish(o, g_ref, lam_init).astype(o_ref.dtype)


def _diff_attn_prompt(dq, dk, dv, lam_p, sub_g, *, batch, seq, lam_init, tq=ATTN_TQ, tk=ATTN_TK):
    nq = seq // tq
    stat = pltpu.VMEM((2, tq, LANES), F32)
    return pl.pallas_call(
        functools.partial(_diff_attn_kernel, tq=tq, tk=tk, lam_init=lam_init),
        grid=(batch, DIFF_HEADS, nq),
        in_specs=[pl.BlockSpec((tq, LANES), lambda b, h, i: (b * nq + i, h)),
                  pl.BlockSpec((seq, LANES), lambda b, h, i: (b, h)),
                  pl.BlockSpec((seq, LANES), lambda b, h, i: (b, h)),
                  pl.BlockSpec((4, DIFF_HEAD_DIM), lambda b, h, i: (0, 0)),
                  pl.BlockSpec((1, DIFF_V_DIM), lambda b, h, i: (0, 0))],
        out_specs=pl.BlockSpec((tq, LANES), lambda b, h, i: (b * nq + i, h)),
        out_shape=jax.ShapeDtypeStruct((batch * seq, DIFF_W), BF16),
        scratch_shapes=[pltpu.VMEM((2, tq, LANES), BF16), stat, stat, stat],
        compiler_params=_cparams(("parallel", "parallel", "arbitrary"), 48),
        name="diff_attn",
    )(dq, dk, dv, lam_p, sub_g)


def _mla_sample_kernel(q_ref, ckvp_ref, kpet_ref, ckvn_ref, kpen_ref, wuk_ref, wuvp_ref, o_ref, *, group):
    ckv_p = ckvp_ref[0, 0].astype(BF16)
    past = ckv_p.shape[0]
    kpe_t = jnp.concatenate([kpet_ref[0, 0].astype(BF16), jnp.zeros((LANES - MLA_ROPE, past), BF16)], axis=0)
    ckv_n = ckvn_ref[...].astype(BF16)
    kpe_n = kpen_ref[...].astype(BF16)
    t = q_ref.shape[0]
    for g0 in range(0, MLA_HEADS, group):
        qlat, qfull, qpe = [], [], []
        for h in range(g0, g0 + group):
            sl = slice(h * HEAD_SLOT, (h + 1) * HEAD_SLOT)
            qh = q_ref[:, sl]
            qlat.append(_dot_nt(qh, wuk_ref[:, sl]).astype(BF16))
            qfull.append(qh)
            qpe.append(jnp.concatenate([qh[:, MLA_NOPE:MLA_NOPE + MLA_ROPE],
                                        jnp.zeros((t, LANES - MLA_ROPE), BF16)], axis=1))
        qlat = jnp.concatenate(qlat, axis=0)
        qfull = jnp.concatenate(qfull, axis=0)
        qpe = jnp.concatenate(qpe, axis=0)
        s_p = _dot_nt(qlat, ckv_p) + _dot(qpe, kpe_t)
        s_n = _dot_nt(qlat, ckv_n) + _dot_nt(qfull, kpe_n)
        m = jnp.maximum(jnp.max(s_p, axis=-1, keepdims=True), jnp.max(s_n, axis=-1, keepdims=True))
        p_p = jnp.exp2(s_p - m)
        p_n = jnp.exp2(s_n - m)
        l = jnp.sum(p_p, axis=-1, keepdims=True) + jnp.sum(p_n, axis=-1, keepdims=True)
        o_lat = ((_dot(p_p.astype(BF16), ckv_p) + _dot(p_n.astype(BF16), ckv_n)) / l).astype(BF16)
        for a in range(0, group, 2):
            h = g0 + a
            o_pair = (_dot(o_lat[a * t:(a + 1) * t], wuvp_ref[:, h * HEAD_SLOT:(h + 1) * HEAD_SLOT])
                      + _dot(o_lat[(a + 1) * t:(a + 2) * t], wuvp_ref[:, (h + 1) * HEAD_SLOT:(h + 2) * HEAD_SLOT]))
            o_ref[:, (h // 2) * LANES:(h // 2 + 1) * LANES] = o_pair.astype(o_ref.dtype)


def _mla_attn_sample(q_pad, ckv_cache, kpe_cache_t, ckv_new, kpe_new, w_uk, w_uv_pad, *, layer, batch, t, group=4):
    past = ckv_cache.shape[2]
    return pl.pallas_call(
        functools.partial(_mla_sample_kernel, group=group),
        grid=(batch,),
        in_specs=[pl.BlockSpec((t, MLA_PAD), lambda b: (b, 0)),
                  pl.BlockSpec((1, 1, past, MLA_KV_RANK), lambda b: (layer, b, 0, 0)),
                  pl.BlockSpec((1, 1, MLA_ROPE, past), lambda b: (layer, b, 0, 0)),
                  pl.BlockSpec((t, MLA_KV_RANK), lambda b: (b, 0)),
                  pl.BlockSpec((t, LANES), lambda b: (b, 0)),
                  _const_spec((MLA_KV_RANK, MLA_PAD)),
                  _const_spec((MLA_KV_RANK, MLA_PAD))],
        out_specs=pl.BlockSpec((t, MLA_HEADS * MLA_V), lambda b: (b, 0)),
        out_shape=jax.ShapeDtypeStruct((batch * t, MLA_HEADS * MLA_V), BF16),
        compiler_params=_cparams(("parallel",), 48),
        name="mla_attn_sample",
    )(q_pad, ckv_cache, kpe_cache_t, ckv_new, kpe_new, w_uk, w_uv_pad)


def _diff_sample_kernel(q_ref, kt_ref, vp_ref, kn_ref, vn_ref, lam_ref, g_ref, o_ref, *, lam_init):
    t = q_ref.shape[0]
    q0, q1 = _split_maps(q_ref[...])
    qm = jnp.concatenate([q0, q1], axis=0)
    s_p = _dot(qm, kt_ref[0, 0, 0].astype(BF16))
    s_n = _dot_nt(qm, kn_ref[...])
    m = jnp.maximum(jnp.max(s_p, axis=-1, keepdims=True), jnp.max(s_n, axis=-1, keepdims=True))
    p_p = jnp.exp2(s_p - m)
    p_n = jnp.exp2(s_n - m)
    inv = 1.0 / (jnp.sum(p_p, axis=-1, keepdims=True) + jnp.sum(p_n, axis=-1, keepdims=True))
    lam = _diff_lambda(lam_ref, lam_init)
    w0 = inv[:t]
    w1 = lam * inv[t:]
    a_p = (p_p[:t] * w0 - p_p[t:] * w1).astype(BF16)
    a_n = (p_n[:t] * w0 - p_n[t:] * w1).astype(BF16)
    o = _dot(a_p, vp_ref[0, 0].astype(BF16)) + _dot(a_n, vn_ref[...])
    o_ref[...] = _diff_finish(o, g_ref, lam_init).astype(o_ref.dtype)


def _diff_attn_sample(dq, k_cache_t, v_cache, dk_new, dv_new, lam_p, sub_g, *, layer, batch, t, lam_init):
    past = v_cache.shape[2]
    new = pl.BlockSpec((t, LANES), lambda b, h: (b, h))
    return pl.pallas_call(
        functools.partial(_diff_sample_kernel, lam_init=lam_init),
        grid=(batch, DIFF_HEADS),
        in_specs=[new,
                  pl.BlockSpec((1, 1, 1, LANES, past), lambda b, h: (layer, b, h, 0, 0)),
                  pl.BlockSpec((1, 1, past, LANES), lambda b, h: (layer, b, 0, h)),
                  new, new,
                  pl.BlockSpec((4, DIFF_HEAD_DIM), lambda b, h: (0, 0)),
                  pl.BlockSpec((1, DIFF_V_DIM), lambda b, h: (0, 0))],
        out_specs=new,
        out_shape=jax.ShapeDtypeStruct((batch * t, DIFF_W), BF16),
        compiler_params=_cparams(("parallel", "parallel"), 40),
        name="diff_attn_sample",
    )(dq, k_cache_t, v_cache, dk_new, dv_new, lam_p, sub_g)


def _conv_kernel(cur_ref, prev_ref, wdw_ref, bdw_ref, lng_ref, lnb_ref, o_ref, xin_ref, y_ref, *, rc, nr):
    tm = cur_ref.shape[0]
    xin_ref[0:CONV_HALO, :] = prev_ref[0]
    xin_ref[CONV_HALO:CONV_HALO + tm, :] = cur_ref[...]
    xin_ref[CONV_HALO + tm:, :] = jnp.zeros((SUBLANES, D_MODEL), F32)
    first = CONV_HALO - CONV_STATE

    def taps(r, carry):
        r0 = pl.multiple_of(r * rc, rc)
        for c in range(D_MODEL // LANES):
            cl = slice(c * LANES, (c + 1) * LANES)
            y = jnp.zeros((rc, LANES), F32) + bdw_ref[:, cl]
            for s in range(SUBLANES):
                z = None
                for w in range(CONV_WIDTH):
                    if (first + w) % SUBLANES == s:
                        start = pl.multiple_of(r0 + (first + w - s), SUBLANES)
                        term = xin_ref[pl.ds(start, rc + SUBLANES), cl] * wdw_ref[w:w + 1, cl]
                        z = term if z is None else z + term
                y = y + z[s:s + rc]
            y_ref[pl.ds(r0, rc), cl] = y
        return carry

    lax.fori_loop(0, tm // rc, taps, 0)

    def norm(r, carry):
        r0 = pl.multiple_of(r * nr, nr)
        y = y_ref[pl.ds(r0, nr), :]
        mu = jnp.mean(y, axis=-1, keepdims=True)
        yc = y - mu
        var = jnp.mean(yc * yc, axis=-1, keepdims=True)
        yn = yc * lax.rsqrt(var + EPS) * lng_ref[...] + lnb_ref[...]
        o_ref[pl.ds(r0, nr), :] = (yn * jax.nn.sigmoid(yn)).astype(o_ref.dtype)
        return carry

    lax.fori_loop(0, tm // nr, norm, 0)


def _conv(glu, prev, w_dw, b_dw, ln_g, ln_b, *, tm):
    rows = glu.shape[0]
    vec = pl.BlockSpec((1, D_MODEL), lambda i: (0, 0))
    return pl.pallas_call(
        functools.partial(_conv_kernel, rc=64, nr=min(128, tm)),
        grid=(rows // tm,),
        in_specs=[pl.BlockSpec((tm, D_MODEL), lambda i: (i, 0)),
                  pl.BlockSpec((1, CONV_HALO, D_MODEL), lambda i: (i, 0, 0)),
                  pl.BlockSpec((CONV_WIDTH, D_MODEL), lambda i: (0, 0)),
                  vec, vec, vec],
        out_specs=pl.BlockSpec((tm, D_MODEL), lambda i: (i, 0)),
        out_shape=jax.ShapeDtypeStruct((rows, D_MODEL), BF16),
        scratch_shapes=[pltpu.VMEM((tm + CONV_HALO + SUBLANES, D_MODEL), F32), pltpu.VMEM((tm, D_MODEL), F32)],
        compiler_params=_cparams(("parallel",), 24),
        name="conv",
    )(glu, prev, w_dw, b_dw, ln_g, ln_b)


def _merge_kernel(x_ref, mod_ref, preg_ref, postg_ref, am_ref, ad_ref, yc_ref,
                  wbg_ref, bbg_ref, wmo_ref, wdo_ref, wpw_ref, bpw_ref, wout_ref, o_ref, *, nseq):
    x = x_ref[...]
    mod = mod_ref[...]
    u = _mod_norm(x, preg_ref[...], mod, nseq).astype(BF16)
    branches = (_dot(am_ref[...], wmo_ref[...]),
                _dot(ad_ref[...], wdo_ref[...]),
                _dot(yc_ref[...], wpw_ref[...]) + bpw_ref[...])
    merged = None
    for k, br in enumerate(branches):
        sl = slice(k * D_MODEL, (k + 1) * D_MODEL)
        gate = jax.nn.sigmoid(_dot(u, wbg_ref[:, sl]) + bbg_ref[:, sl])
        merged = gate * br if merged is None else merged + gate * br
    o = _dot(merged.astype(BF16), wout_ref[...])
    o_ref[...] = _gated_residual(x, o, postg_ref[...], mod, nseq, 1.0)


def _merge(x, mod, pre_g, post_g, a_mla, a_diff, y_conv, w_bg, b_bg, w_mo, w_do, w_pw, b_pw, w_out, *, srows, tm=512):
    rows = x.shape[0]
    nseq, mod_spec = _mod_spec(tm, srows)
    row = pl.BlockSpec((tm, D_MODEL), lambda i: (i, 0))
    sq = _const_spec((D_MODEL, D_MODEL))
    vec = _const_spec((1, D_MODEL))
    return pl.pallas_call(
        functools.partial(_merge_kernel, nseq=nseq),
        grid=(rows // tm,),
        in_specs=[row, mod_spec, vec, vec, row, row, row,
                  _const_spec((D_MODEL, 3 * D_MODEL)), _const_spec((1, 3 * D_MODEL)), sq, sq, sq, vec, sq],
        out_specs=row,
        out_shape=jax.ShapeDtypeStruct((rows, D_MODEL), F32),
        compiler_params=_cparams(("parallel",), 56),
        name="merge",
    )(x, mod, pre_g, post_g, a_mla, a_diff, y_conv, w_bg, b_bg, w_mo, w_do, w_pw, b_pw, w_out)


def _rope_angles(pos, rot_dim):
    half = rot_dim // 2
    inv_freq = ROPE_THETA ** (-jnp.arange(half, dtype=F32) / half)
    ang = pos.astype(F32)[:, None] * inv_freq[None, :]
    return jnp.cos(ang), jnp.sin(ang)


def _mla_tables(pos, scale):
    cos, sin = _rope_angles(pos, MLA_ROPE)
    t = pos.shape[0]
    z = lambda n: jnp.zeros((t, n), F32)
    pad = HEAD_SLOT - MLA_NOPE - MLA_ROPE
    c = jnp.concatenate([jnp.ones((t, MLA_NOPE), F32), cos, cos, z(pad)], axis=1)
    s1 = jnp.concatenate([z(MLA_NOPE), -sin, z(MLA_ROPE // 2), z(pad)], axis=1)
    s2 = jnp.concatenate([z(MLA_NOPE), z(MLA_ROPE // 2), sin, z(pad)], axis=1)
    return jnp.stack([c, s1, s2]) * scale


def _diff_tables(pos, scale):
    cos, sin = _rope_angles(pos, DIFF_HEAD_DIM)
    z = jnp.zeros_like(sin)
    c = jnp.concatenate([cos, cos, cos, cos], axis=1)
    s1 = jnp.concatenate([-sin, z, -sin, z], axis=1)
    s2 = jnp.concatenate([z, sin, z, sin], axis=1)
    return jnp.stack([c, s1, s2]) * scale


def _tables(pos, reps):
    tabs = (_mla_tables(pos, MLA_SCALE * LOG2E), _mla_tables(pos, 1.0),
            _diff_tables(pos, DIFF_SCALE * LOG2E), _diff_tables(pos, 1.0))
    return tuple(jnp.tile(t, (1, reps, 1)) for t in tabs)


def _pad_heads(w, heads, width):
    k = w.shape[0]
    w = w.reshape(k, heads, width)
    return jnp.pad(w, ((0, 0), (0, 0), (0, HEAD_SLOT - width))).reshape(k, heads * HEAD_SLOT)


def _pad_pairs(w):
    k = w.shape[0]
    w = w.reshape(k, MLA_HEADS // 2, 2, MLA_V)
    z = jnp.zeros_like(w[:, :, 0])
    lo = jnp.concatenate([w[:, :, 0], z], axis=-1)
    hi = jnp.concatenate([z, w[:, :, 1]], axis=-1)
    return jnp.stack([lo, hi], axis=2).reshape(k, MLA_PAD)


def _permute_w_in(w):
    s0 = MLA_Q_RANK + MLA_KV_RANK
    z = lambda n: jnp.zeros((w.shape[0], n), w.dtype)
    return jnp.concatenate([w[:, :s0], z(MLA_NOPE), w[:, s0:s0 + MLA_ROPE], z(HEAD_SLOT - MLA_NOPE - MLA_ROPE),
                            w[:, s0 + MLA_ROPE:]], axis=1)


def _conv_prev_prompt(glu, batch, seq, tm):
    g = glu.reshape(batch, seq // tm, tm, D_MODEL)[:, :-1, tm - CONV_HALO:, :]
    g = jnp.pad(g, ((0, 0), (1, 0), (0, 0), (0, 0)))
    return g.reshape(batch * (seq // tm), CONV_HALO, D_MODEL)


def kernel(x_prompt, x_sample, c_prompt, c_sample, cache_mla_ckv, cache_mla_kpe, cache_diff_k, cache_diff_v, state_conv, ada_w, ada_b, ffn1_pre_g, ffn1_post_g, ffn1_w_gate, ffn1_w_up, ffn1_w_down, mix_pre_g, mix_post_g, w_in, mla_q_norm_g, mla_w_uq, mla_kv_norm_g, mla_w_uk, mla_w_uv, mla_w_o, diff_lq1, diff_lk1, diff_lq2, diff_lk2, diff_subln_g, diff_w_o, conv_w_dw, conv_b_dw, conv_ln_g, conv_ln_b, conv_w_pw2, conv_b_pw2, w_branch_gate, b_branch_gate, w_out, ffn2_pre_g, ffn2_post_g, ffn2_w_gate, ffn2_w_up, ffn2_w_down):
    pb, pt, _ = x_prompt.shape
    sb, st, _ = x_sample.shape
    past = cache_mla_ckv.shape[2]
    assert st == CHUNK and past % CHUNK == 0, "sample rows must be exactly the chunk that follows the cache"

    mods = _ada(jnp.concatenate([c_prompt, c_sample], axis=0), ada_w, ada_b)
    mods = mods.reshape(DEPTH, pb + sb, 3, 3, D_MODEL)

    mix_tm = 256
    tabs_p = _tables(jnp.arange(pt, dtype=jnp.int32), 1)
    tabs_s = _tables(past + jnp.arange(st, dtype=jnp.int32), mix_tm // st)

    kpe_cache_t = jnp.swapaxes(cache_mla_kpe, 2, 3)
    dk_cache_t = jnp.transpose(cache_diff_k, (0, 1, 3, 4, 5, 2)).reshape(DEPTH, sb, DIFF_HEADS, DIFF_V_DIM, past)
    dv_cache = cache_diff_v.reshape(DEPTH, sb, past, DIFF_W)

    xs = {"p": x_prompt.reshape(pb * pt, D_MODEL), "s": x_sample.reshape(sb * st, D_MODEL)}
    srows = {"p": pt, "s": st}
    states = {"p": [], "s": []}
    diff_state = None
    bf = lambda a: a.astype(BF16)
    vec = lambda a: a.reshape(1, -1)

    for l in range(DEPTH):
        lam_init = 0.8 - 0.6 * math.exp(-0.3 * l)
        w = dict(
            f1=(vec(ffn1_pre_g[l]), vec(ffn1_post_g[l]), bf(ffn1_w_gate[l]), bf(ffn1_w_up[l]), bf(ffn1_w_down[l])),
            f2=(vec(ffn2_pre_g[l]), vec(ffn2_post_g[l]), bf(ffn2_w_gate[l]), bf(ffn2_w_up[l]), bf(ffn2_w_down[l])),
            w_in=bf(_permute_w_in(w_in[l])),
            w_uq=bf(_pad_heads(mla_w_uq[l], MLA_HEADS, MLA_NOPE + MLA_ROPE)),
            w_uk=bf(_pad_heads(mla_w_uk[l], MLA_HEADS, MLA_NOPE)),
            w_uv=bf(mla_w_uv[l]),
            w_uv_pad=bf(_pad_pairs(mla_w_uv[l])),
            lam=jnp.stack([diff_lq1[l], diff_lk1[l], diff_lq2[l], diff_lk2[l]]),
        )
        for key in ("p", "s"):
            x = xs[key]
            sr = srows[key]
            mod = mods[l, :pb] if key == "p" else mods[l, pb:]
            x = _ffn(x, mod[:, 0], *w["f1"], srows=sr)
            outs = _mix_in(x, mod[:, 1], vec(mix_pre_g[l]), w["w_in"], vec(mla_q_norm_g[l]), w["w_uq"],
                           vec(mla_kv_norm_g[l]), w["w_uk"], w["w_uv"], tabs_p if key == "p" else tabs_s,
                           srows=sr, layer=l, state=diff_state if key == "p" else None, tm=mix_tm)
            q_pad, ckv, kpe, dq, dk_b, dv_b, glu = outs[:7]
            if key == "p":
                k_pad, v_pad, *diff_state = outs[7:]
                dk_f = dv_f = None
                a_mla = _mla_attn_prompt(q_pad, k_pad, v_pad, batch=pb, seq=pt)
                a_diff = _diff_attn_prompt(dq, dk_b, dv_b, w["lam"], vec(diff_subln_g[l]),
                                           batch=pb, seq=pt, lam_init=lam_init)
                conv_tm = 256
                prev = _conv_prev_prompt(glu, pb, pt, conv_tm)
                new_conv = glu.reshape(pb, pt, D_MODEL)[:, pt - CONV_STATE:, :]
                nb, nt = pb, pt
            else:
                dk_f, dv_f = outs[7:]
                a_mla = _mla_attn_sample(q_pad, cache_mla_ckv, kpe_cache_t, ckv, kpe, w["w_uk"],
                                         w["w_uv_pad"], layer=l, batch=sb, t=st)
                a_diff = _diff_attn_sample(dq, dk_cache_t, dv_cache, dk_b, dv_b, w["lam"],
                                           vec(diff_subln_g[l]), layer=l, batch=sb, t=st, lam_init=lam_init)
                conv_tm = st
                prev = jnp.pad(state_conv[l], ((0, 0), (CONV_HALO - CONV_STATE, 0), (0, 0)))
                xin = jnp.concatenate([state_conv[l], glu.reshape(sb, st, D_MODEL)], axis=1)
                new_conv = xin[:, st:, :]
                nb, nt = sb, st
            y_conv = _conv(glu, prev, conv_w_dw[l], vec(conv_b_dw[l]), vec(conv_ln_g[l]), vec(conv_ln_b[l]),
                           tm=conv_tm)
            x = _merge(x, mod[:, 1], vec(mix_pre_g[l]), vec(mix_post_g[l]), a_mla, a_diff, y_conv,
                       bf(w_branch_gate[l]), vec(b_branch_gate[l]), bf(mla_w_o[l]), bf(diff_w_o[l]),
                       bf(conv_w_pw2[l]), vec(conv_b_pw2[l]), bf(w_out[l]), srows=sr)
            x = _ffn(x, mod[:, 2], *w["f2"], srows=sr)
            xs[key] = x
            states[key].append((
                ckv.reshape(nb, nt, MLA_KV_RANK),
                kpe[:, MLA_NOPE:MLA_NOPE + MLA_ROPE].reshape(nb, nt, MLA_ROPE),
                None if dk_f is None else dk_f.reshape(nb, nt, DIFF_HEADS, 2, DIFF_HEAD_DIM),
                None if dv_f is None else dv_f.reshape(nb, nt, DIFF_HEADS, DIFF_V_DIM),
                new_conv))

    stack = lambda key, k: jnp.stack([s[k] for s in states[key]])
    dk_t, dv_s = diff_state
    dk_prompt = jnp.transpose(dk_t.reshape(DEPTH, pb, DIFF_HEADS, 2, DIFF_HEAD_DIM, pt), (0, 1, 5, 2, 3, 4))
    dv_prompt = dv_s.reshape(DEPTH, pb, pt, DIFF_HEADS, DIFF_V_DIM)
    return (xs["p"].reshape(pb, pt, D_MODEL), xs["s"].reshape(sb, st, D_MODEL),
            stack("p", 0), stack("p", 1), dk_prompt, dv_prompt, stack("p", 4),
            *(stack("s", k) for k in range(5)))
```

```python
import functools
import math

import jax
import jax.numpy as jnp
from jax import lax
from jax.experimental import pallas as pl
from jax.experimental.pallas import tpu as pltpu

D_MODEL = 1024
DEPTH = 2
CHUNK = 64
ROPE_THETA = 10000.0
EPS = 1e-6
MACARON_W = 0.5

MLA_HEADS = 16
MLA_NOPE = 64
MLA_ROPE = 32
MLA_V = 64
MLA_KV_RANK = 256
MLA_Q_RANK = 512
MLA_SCALE = (MLA_NOPE + MLA_ROPE) ** -0.5

DIFF_HEADS = 8
DIFF_HEAD_DIM = 64
DIFF_V_DIM = 2 * DIFF_HEAD_DIM
DIFF_SCALE = DIFF_HEAD_DIM ** -0.5
LOG2E = math.log2(math.e)

CONV_WIDTH = 31
CONV_STATE = CONV_WIDTH - 1
CONV_HALO = 32
D_FF = 2816
N_ADA = 9

LANES = 128
SUBLANES = 8
HEAD_SLOT = LANES
MLA_PAD = MLA_HEADS * HEAD_SLOT
MLA_ONE_LANE = (MLA_V, 0)
DIFF_W = DIFF_HEADS * 2 * DIFF_HEAD_DIM
MIB = 1024 * 1024

C_CQ = 0
C_CKV = C_CQ + MLA_Q_RANK
C_KPE = C_CKV + MLA_KV_RANK
C_DQ = C_KPE + HEAD_SLOT
C_DK = C_DQ + DIFF_W
C_DV = C_DK + DIFF_W
C_GA = C_DV + DIFF_W
C_GB = C_GA + D_MODEL
C_END = C_GB + D_MODEL

F32 = jnp.float32
BF16 = jnp.bfloat16


def _cparams(sem, vmem_mib):
    return pltpu.CompilerParams(dimension_semantics=sem, vmem_limit_bytes=int(vmem_mib * MIB))


def _dot(a, b):
    return jnp.dot(a, b, preferred_element_type=F32)


def _dot_nt(a, b):
    return lax.dot_general(a, b, (((1,), (1,)), ((), ())), preferred_element_type=F32)


def _rms(x):
    return x * lax.rsqrt(jnp.mean(x * x, axis=-1, keepdims=True) + EPS)


def _mod_norm(x, g, mod, nseq):
    tm = x.shape[0]
    xn = _rms(x) * g
    sh = mod[:, 0:1, :]
    sc = mod[:, 1:2, :]
    if nseq == 1:
        return xn * (1.0 + sc[0]) + sh[0]
    xn = xn.reshape(nseq, tm // nseq, x.shape[1])
    return (xn * (1.0 + sc) + sh).reshape(tm, x.shape[1])


def _gated_residual(x, y, post_g, mod, nseq, weight):
    tm = x.shape[0]
    yn = _rms(y) * post_g
    gate = mod[:, 2:3, :]
    if nseq == 1:
        return x + weight * gate[0] * yn
    yn = yn.reshape(nseq, tm // nseq, x.shape[1])
    return x + (weight * gate * yn).reshape(tm, x.shape[1])


def _mod_spec(tm, srows):
    if tm <= srows:
        per = srows // tm
        return 1, pl.BlockSpec((1, 3, D_MODEL), lambda *idx: (idx[0] // per, 0, 0))
    nseq = tm // srows
    return nseq, pl.BlockSpec((nseq, 3, D_MODEL), lambda *idx: (idx[0], 0, 0))


def _const_spec(shape):
    nd = len(shape)
    return pl.BlockSpec(shape, lambda *idx: (0,) * nd, pipeline_mode=pl.Buffered(1))


def _ada_kernel(c_ref, w_ref, b_ref, o_ref):
    c = c_ref[...]
    a = (c * jax.nn.sigmoid(c)).astype(BF16)
    o_ref[0] = _dot(a, w_ref[0].astype(BF16)) + b_ref[0]


def _ada(c_all, ada_w, ada_b):
    nb = c_all.shape[0]
    tn = 1024
    n = N_ADA * D_MODEL
    return pl.pallas_call(
        _ada_kernel,
        grid=(DEPTH, n // tn),
        in_specs=[pl.BlockSpec((nb, D_MODEL), lambda l, j: (0, 0)),
                  pl.BlockSpec((1, D_MODEL, tn), lambda l, j: (l, 0, j)),
                  pl.BlockSpec((1, 1, tn), lambda l, j: (l, 0, j))],
        out_specs=pl.BlockSpec((1, nb, tn), lambda l, j: (l, 0, j)),
        out_shape=jax.ShapeDtypeStruct((DEPTH, nb, n), F32),
        compiler_params=_cparams(("parallel", "parallel"), 24),
        name="ada",
    )(c_all, ada_w, ada_b.reshape(DEPTH, 1, n))


def _ffn_kernel(x_ref, mod_ref, preg_ref, postg_ref, wg_ref, wu_ref, wd_ref, o_ref, *, nseq):
    x = x_ref[...]
    mod = mod_ref[...]
    h = _mod_norm(x, preg_ref[...], mod, nseq).astype(BF16)
    g = _dot(h, wg_ref[...])
    u = _dot(h, wu_ref[...])
    a = (g * jax.nn.sigmoid(g) * u).astype(BF16)
    o_ref[...] = _gated_residual(x, _dot(a, wd_ref[...]), postg_ref[...], mod, nseq, MACARON_W)


def _ffn(x, mod, pre_g, post_g, wg, wu, wd, *, srows, tm=512):
    rows = x.shape[0]
    nseq, mod_spec = _mod_spec(tm, srows)
    row_spec = pl.BlockSpec((tm, D_MODEL), lambda i: (i, 0))
    vec_spec = _const_spec((1, D_MODEL))
    return pl.pallas_call(
        functools.partial(_ffn_kernel, nseq=nseq),
        grid=(rows // tm,),
        in_specs=[row_spec, mod_spec, vec_spec, vec_spec,
                  _const_spec((D_MODEL, D_FF)), _const_spec((D_MODEL, D_FF)), _const_spec((D_FF, D_MODEL))],
        out_specs=row_spec,
        out_shape=jax.ShapeDtypeStruct((rows, D_MODEL), F32),
        compiler_params=_cparams(("parallel",), 56),
        name="ffn",
    )(x, mod, pre_g, post_g, wg, wu, wd)


def _rope_store(x, tbl_ref, shift, out_refs, t_ref=None):
    n = x.shape[1]
    c, s1, s2 = tbl_ref[0], tbl_ref[1], tbl_ref[2]
    xm = pltpu.roll(x, n - shift, 1)
    xp = pltpu.roll(x, shift, 1)
    for b in range(n // LANES):
        sl = slice(b * LANES, (b + 1) * LANES)
        r = x[:, sl] * c + xm[:, sl] * s1 + xp[:, sl] * s2
        for ref in out_refs:
            ref[:, sl] = r.astype(ref.dtype)
        if t_ref is not None:
            t_ref[sl, :] = r.T


N_MIX_IN = 13


def _mix_in_kernel(*refs, nseq, prompt):
    (x_ref, mod_ref, preg_ref, win_ref, qg_ref, wuq_ref, kvg_ref, wuk_ref, wuv_ref,
     tmq_ref, tmk_ref, tdq_ref, tdk_ref) = refs[:N_MIX_IN]
    outs = refs[len(refs) - (11 if prompt else 9):]
    qpad_ref, ckv_ref, kpe_ref, dq_ref, dkb_ref, dvb_ref, glu_ref = outs[:7]
    if prompt:
        kpad_ref, vpad_ref, dkt_ref, dvs_ref = outs[7:]
        dk_outs, dk_t, dvf_ref = (dkb_ref,), dkt_ref.at[0, 0], dvs_ref.at[0]
    else:
        dkf_ref, dvf_ref = outs[7:]
        dk_outs, dk_t = (dkf_ref, dkb_ref), None
    u = _mod_norm(x_ref[...], preg_ref[...], mod_ref[...], nseq).astype(BF16)

    def proj(a, b):
        return _dot(u, win_ref[:, a:b])

    cqn = (_rms(proj(C_CQ, C_CKV)) * qg_ref[...]).astype(BF16)
    _rope_store(_dot(cqn, wuq_ref[...]), tmq_ref, MLA_ROPE // 2, (qpad_ref,))

    ckvn = _rms(proj(C_CKV, C_KPE)) * kvg_ref[...]
    ckv_ref[...] = ckvn
    _rope_store(proj(C_KPE, C_DQ), tmk_ref, MLA_ROPE // 2, (kpe_ref,))

    if prompt:
        ckvb = ckvn.astype(BF16)
        kpe = kpe_ref[...]
        kn = _dot(ckvb, wuk_ref[...])
        for h in range(MLA_HEADS):
            sl = slice(h * HEAD_SLOT, (h + 1) * HEAD_SLOT)
            kpad_ref[:, sl] = (kn[:, sl] + kpe).astype(BF16)
        v = _dot(ckvb, wuv_ref[...])
        lane = lax.broadcasted_iota(jnp.int32, (1, LANES), 1)
        low = lane < MLA_V
        one_even = (lane == MLA_ONE_LANE[0]).astype(F32)
        one_odd = (lane == MLA_ONE_LANE[1]).astype(F32)
        for p in range(MLA_HEADS // 2):
            vp = v[:, p * LANES:(p + 1) * LANES]
            vpad_ref[:, (2 * p) * LANES:(2 * p + 1) * LANES] = jnp.where(low, vp, one_even).astype(BF16)
            vpad_ref[:, (2 * p + 1) * LANES:(2 * p + 2) * LANES] = jnp.where(low, one_odd, vp).astype(BF16)

    _rope_store(proj(C_DQ, C_DK), tdq_ref, DIFF_HEAD_DIM // 2, (dq_ref,))
    _rope_store(proj(C_DK, C_DV), tdk_ref, DIFF_HEAD_DIM // 2, dk_outs, dk_t)
    dv = proj(C_DV, C_GA)
    dvf_ref[...] = dv
    dvb_ref[...] = dv.astype(BF16)

    glu_ref[...] = proj(C_GA, C_GB) * jax.nn.sigmoid(proj(C_GB, C_END))


def _mix_in(x, mod, pre_g, w_in, q_g, w_uq, kv_g, w_uk, w_uv, tables, *, srows, layer, state=None, tm=256):
    rows = x.shape[0]
    prompt = srows > tm
    nseq, mod_spec = _mod_spec(tm, srows)
    ntab = tables[0].shape[1] // tm
    row = lambda n: pl.BlockSpec((tm, n), lambda i: (i, 0))
    tab_spec = pl.BlockSpec((3, tm, LANES), lambda i: (0, i % ntab, 0))
    out_cols = [(MLA_PAD, BF16), (MLA_KV_RANK, F32), (LANES, F32), (DIFF_W, BF16), (DIFF_W, BF16), (DIFF_W, BF16),
                (D_MODEL, F32)]
    out_cols += [(MLA_PAD, BF16), (MLA_PAD, BF16)] if prompt else [(DIFF_W, F32), (DIFF_W, F32)]
    out_specs = [row(n) for n, _ in out_cols]
    out_shape = [jax.ShapeDtypeStruct((rows, n), dt) for n, dt in out_cols]
    in_specs = [row(D_MODEL), mod_spec, _const_spec((1, D_MODEL)), _const_spec((D_MODEL, C_END)),
                _const_spec((1, MLA_Q_RANK)), _const_spec((MLA_Q_RANK, MLA_PAD)),
                _const_spec((1, MLA_KV_RANK)), _const_spec((MLA_KV_RANK, MLA_PAD)),
                _const_spec((MLA_KV_RANK, MLA_HEADS * MLA_V)),
                tab_spec, tab_spec, tab_spec, tab_spec]
    operands = [x, mod, pre_g, w_in, q_g, w_uq, kv_g, w_uk, w_uv, *tables]
    assert len(operands) == N_MIX_IN
    aliases = {}
    if prompt:
        per = srows // tm
        out_specs += [pl.BlockSpec((1, 1, DIFF_W, tm), lambda i: (layer, i // per, 0, i % per)),
                      pl.BlockSpec((1, tm, DIFF_W), lambda i: (layer, i, 0))]
        out_shape += [jax.ShapeDtypeStruct((DEPTH, rows // srows, DIFF_W, srows), F32),
                      jax.ShapeDtypeStruct((DEPTH, rows, DIFF_W), F32)]
        if state is not None:
            operands += list(state)
            in_specs += [pl.BlockSpec(memory_space=pl.ANY)] * len(state)
            aliases = {N_MIX_IN + k: len(out_shape) - len(state) + k for k in range(len(state))}
    return pl.pallas_call(
        functools.partial(_mix_in_kernel, nseq=nseq, prompt=prompt),
        grid=(rows // tm,),
        in_specs=in_specs,
        out_specs=out_specs,
        out_shape=out_shape,
        input_output_aliases=aliases,
        compiler_params=_cparams(("parallel",), 56),
        name="mix_in",
    )(*operands)


ATTN_TQ = 2048
ATTN_TK = 512


def _rep_lanes(x, n):
    return jnp.concatenate([x] * n, axis=1) if n > 1 else x


def _online_step(s, m_ref, l_ref, rows):
    m_prev = m_ref[rows, :]
    m_new = jnp.maximum(m_prev, jnp.max(s, axis=-1, keepdims=True))
    alpha = jnp.exp2(m_prev - m_new)
    p = jnp.exp2((s - _rep_lanes(m_new, s.shape[1] // LANES)).astype(BF16))
    l_ref[rows, :] = alpha * l_ref[rows, :] + jnp.sum(p.astype(F32), axis=-1, keepdims=True)
    m_ref[rows, :] = m_new
    return p, alpha


def _causal_sweep(i, tq, tk, tile_fn):
    per = tq // tk

    def body(j, carry):
        tile_fn(j, 0, None)
        return carry

    lax.fori_loop(0, i * per, body, 0)
    for dj in range(per):
        tile_fn(i * per + dj, dj * tk, dj)


def _diag_mask(s, row0, dj, tk):
    qc = (lax.broadcasted_iota(jnp.int32, s.shape, 0) + row0) // CHUNK
    kc = (lax.broadcasted_iota(jnp.int32, s.shape, 1) + dj * tk) // CHUNK
    return jnp.where(kc <= qc, s, -jnp.inf)


def _mla_attn_kernel(q_ref, k_ref, v_ref, o_ref, m_ref, acc_ref, *, tq, tk):
    i = pl.program_id(2)
    m_ref[...] = jnp.full_like(m_ref, -jnp.inf)
    acc_ref[...] = jnp.zeros_like(acc_ref)

    def tile(j, row0, dj):
        keys = pl.ds(pl.multiple_of(j * tk, tk), tk)
        rows = pl.ds(row0, tq - row0)
        for hh in range(2):
            sl = slice(hh * HEAD_SLOT, (hh + 1) * HEAD_SLOT)
            s = _dot_nt(q_ref[rows, sl], k_ref[keys, sl])
            if dj is not None:
                s = _diag_mask(s, row0, dj, tk)
            m_prev = m_ref[hh, rows, :]
            m_new = jnp.maximum(m_prev, jnp.max(s, axis=-1, keepdims=True))
            p = jnp.exp2((s - _rep_lanes(m_new, tk // LANES)).astype(BF16))
            m_ref[hh, rows, :] = m_new
            acc_ref[hh, rows, :] = acc_ref[hh, rows, :] * jnp.exp2(m_prev - m_new) + _dot(p, v_ref[keys, sl])

    _causal_sweep(i, tq, tk, tile)
    low = lax.broadcasted_iota(jnp.int32, (1, LANES), 1) < MLA_V
    a0, a1 = acc_ref[0], acc_ref[1]
    o_ref[...] = jnp.where(low, a0 / a0[:, MLA_ONE_LANE[0]:MLA_ONE_LANE[0] + 1],
                           a1 / a1[:, MLA_ONE_LANE[1]:MLA_ONE_LANE[1] + 1]).astype(o_ref.dtype)


def _mla_attn_prompt(q_pad, k_pad, v_pad, *, batch, seq, tq=ATTN_TQ, tk=ATTN_TK):
    nq = seq // tq
    pair = 2 * HEAD_SLOT
    stat = pltpu.VMEM((2, tq, LANES), F32)
    return pl.pallas_call(
        functools.partial(_mla_attn_kernel, tq=tq, tk=tk),
        grid=(batch, MLA_HEADS // 2, nq),
        in_specs=[pl.BlockSpec((tq, pair), lambda b, p, i: (b * nq + i, p)),
                  pl.BlockSpec((seq, pair), lambda b, p, i: (b, p)),
                  pl.BlockSpec((seq, pair), lambda b, p, i: (b, p))],
        out_specs=pl.BlockSpec((tq, LANES), lambda b, p, i: (b * nq + i, p)),
        out_shape=jax.ShapeDtypeStruct((batch * seq, MLA_HEADS * MLA_V), BF16),
        scratch_shapes=[stat, stat],
        compiler_params=_cparams(("parallel", "parallel", "arbitrary"), 48),
        name="mla_attn",
    )(q_pad, k_pad, v_pad)


def _diff_lambda(lam_ref, lam_init):
    lp = lam_ref[...]
    s1 = jnp.sum(lp[0:1] * lp[1:2], axis=-1, keepdims=True)
    s2 = jnp.sum(lp[2:3] * lp[3:4], axis=-1, keepdims=True)
    return jnp.exp(s1) - jnp.exp(s2) + lam_init


def _diff_finish(o, g_ref, lam_init):
    return _rms(o) * g_ref[...] * (1.0 - lam_init)


def _split_maps(q):
    low = lax.broadcasted_iota(jnp.int32, (1, LANES), 1) < DIFF_HEAD_DIM
    zero = jnp.zeros_like(q)
    return jnp.where(low, q, zero), jnp.where(low, zero, q)


def _diff_attn_kernel(q_ref, k_ref, v_ref, lam_ref, g_ref, o_ref, qs_ref, m_ref, l_ref, acc_ref, *, tq, tk, lam_init):
    i = pl.program_id(2)
    m_ref[...] = jnp.full_like(m_ref, -jnp.inf)
    l_ref[...] = jnp.zeros_like(l_ref)
    acc_ref[...] = jnp.zeros_like(acc_ref)
    qs_ref[0], qs_ref[1] = _split_maps(q_ref[...])

    def tile(j, row0, dj):
        keys = pl.ds(pl.multiple_of(j * tk, tk), tk)
        rows = pl.ds(row0, tq - row0)
        k = k_ref[keys, :]
        v = v_ref[keys, :]
        for mp in range(2):
            s = _dot_nt(qs_ref[mp, rows, :], k)
            if dj is not None:
                s = _diag_mask(s, row0, dj, tk)
            p, alpha = _online_step(s, m_ref.at[mp], l_ref.at[mp], rows)
            acc_ref[mp, rows, :] = acc_ref[mp, rows, :] * alpha + _dot(p, v)

    _causal_sweep(i, tq, tk, tile)
    lam = _diff_lambda(lam_ref, lam_init)
    o = acc_ref[0] / l_ref[0] - lam * (acc_ref[1] / l_ref[1])
    o_ref[...] = _diff_finish(o, g_ref, lam_init).astype(o_ref.dtype)


def _diff_attn_prompt(dq, dk, dv, lam_p, sub_g, *, batch, seq, lam_init, tq=ATTN_TQ, tk=ATTN_TK):
    nq = seq // tq
    stat = pltpu.VMEM((2, tq, LANES), F32)
    return pl.pallas_call(
        functools.partial(_diff_attn_kernel, tq=tq, tk=tk, lam_init=lam_init),
        grid=(batch, DIFF_HEADS, nq),
        in_specs=[pl.BlockSpec((tq, LANES), lambda b, h, i: (b * nq + i, h)),
                  pl.BlockSpec((seq, LANES), lambda b, h, i: (b, h)),
                  pl.BlockSpec((seq, LANES), lambda b, h, i: (b, h)),
                  pl.BlockSpec((4, DIFF_HEAD_DIM), lambda b, h, i: (0, 0)),
                  pl.BlockSpec((1, DIFF_V_DIM), lambda b, h, i: (0, 0))],
        out_specs=pl.BlockSpec((tq, LANES), lambda b, h, i: (b * nq + i, h)),
        out_shape=jax.ShapeDtypeStruct((batch * seq, DIFF_W), BF16),
        scratch_shapes=[pltpu.VMEM((2, tq, LANES), BF16), stat, stat, stat],
        compiler_params=_cparams(("parallel", "parallel", "arbitrary"), 48),
        name="diff_attn",
    )(dq, dk, dv, lam_p, sub_g)


def _mla_sample_kernel(q_ref, ckvp_ref, kpet_ref, ckvn_ref, kpen_ref, wuk_ref, wuvp_ref, o_ref, *, group):
    ckv_p = ckvp_ref[0, 0].astype(BF16)
    past = ckv_p.shape[0]
    kpe_t = jnp.concatenate([kpet_ref[0, 0].astype(BF16), jnp.zeros((LANES - MLA_ROPE, past), BF16)], axis=0)
    ckv_n = ckvn_ref[...].astype(BF16)
    kpe_n = kpen_ref[...].astype(BF16)
    t = q_ref.shape[0]
    for g0 in range(0, MLA_HEADS, group):
        qlat, qfull, qpe = [], [], []
        for h in range(g0, g0 + group):
            sl = slice(h * HEAD_SLOT, (h + 1) * HEAD_SLOT)
            qh = q_ref[:, sl]
            qlat.append(_dot_nt(qh, wuk_ref[:, sl]).astype(BF16))
            qfull.append(qh)
            qpe.append(jnp.concatenate([qh[:, MLA_NOPE:MLA_NOPE + MLA_ROPE],
                                        jnp.zeros((t, LANES - MLA_ROPE), BF16)], axis=1))
        qlat = jnp.concatenate(qlat, axis=0)
        qfull = jnp.concatenate(qfull, axis=0)
        qpe = jnp.concatenate(qpe, axis=0)
        s_p = _dot_nt(qlat, ckv_p) + _dot(qpe, kpe_t)
        s_n = _dot_nt(qlat, ckv_n) + _dot_nt(qfull, kpe_n)
        m = jnp.maximum(jnp.max(s_p, axis=-1, keepdims=True), jnp.max(s_n, axis=-1, keepdims=True))
        p_p = jnp.exp2(s_p - m)
        p_n = jnp.exp2(s_n - m)
        l = jnp.sum(p_p, axis=-1, keepdims=True) + jnp.sum(p_n, axis=-1, keepdims=True)
        o_lat = ((_dot(p_p.astype(BF16), ckv_p) + _dot(p_n.astype(BF16), ckv_n)) / l).astype(BF16)
        for a in range(0, group, 2):
            h = g0 + a
            o_pair = (_dot(o_lat[a * t:(a + 1) * t], wuvp_ref[:, h * HEAD_SLOT:(h + 1) * HEAD_SLOT])
                      + _dot(o_lat[(a + 1) * t:(a + 2) * t], wuvp_ref[:, (h + 1) * HEAD_SLOT:(h + 2) * HEAD_SLOT]))
            o_ref[:, (h // 2) * LANES:(h // 2 + 1) * LANES] = o_pair.astype(o_ref.dtype)


def _mla_attn_sample(q_pad, ckv_cache, kpe_cache_t, ckv_new, kpe_new, w_uk, w_uv_pad, *, layer, batch, t, group=4):
    past = ckv_cache.shape[2]
    return pl.pallas_call(
        functools.partial(_mla_sample_kernel, group=group),
        grid=(batch,),
        in_specs=[pl.BlockSpec((t, MLA_PAD), lambda b: (b, 0)),
                  pl.BlockSpec((1, 1, past, MLA_KV_RANK), lambda b: (layer, b, 0, 0)),
                  pl.BlockSpec((1, 1, MLA_ROPE, past), lambda b: (layer, b, 0, 0)),
                  pl.BlockSpec((t, MLA_KV_RANK), lambda b: (b, 0)),
                  pl.BlockSpec((t, LANES), lambda b: (b, 0)),
                  _const_spec((MLA_KV_RANK, MLA_PAD)),
                  _const_spec((MLA_KV_RANK, MLA_PAD))],
        out_specs=pl.BlockSpec((t, MLA_HEADS * MLA_V), lambda b: (b, 0)),
        out_shape=jax.ShapeDtypeStruct((batch * t, MLA_HEADS * MLA_V), BF16),
        compiler_params=_cparams(("parallel",), 48),
        name="mla_attn_sample",
    )(q_pad, ckv_cache, kpe_cache_t, ckv_new, kpe_new, w_uk, w_uv_pad)


def _diff_sample_kernel(q_ref, kt_ref, vp_ref, kn_ref, vn_ref, lam_ref, g_ref, o_ref, *, lam_init):
    t = q_ref.shape[0]
    q0, q1 = _split_maps(q_ref[...])
    qm = jnp.concatenate([q0, q1], axis=0)
    s_p = _dot(qm, kt_ref[0, 0, 0].astype(BF16))
    s_n = _dot_nt(qm, kn_ref[...])
    m = jnp.maximum(jnp.max(s_p, axis=-1, keepdims=True), jnp.max(s_n, axis=-1, keepdims=True))
    p_p = jnp.exp2(s_p - m)
    p_n = jnp.exp2(s_n - m)
    inv = 1.0 / (jnp.sum(p_p, axis=-1, keepdims=True) + jnp.sum(p_n, axis=-1, keepdims=True))
    lam = _diff_lambda(lam_ref, lam_init)
    w0 = inv[:t]
    w1 = lam * inv[t:]
    a_p = (p_p[:t] * w0 - p_p[t:] * w1).astype(BF16)
    a_n = (p_n[:t] * w0 - p_n[t:] * w1).astype(BF16)
    o = _dot(a_p, vp_ref[0, 0]) + _dot(a_n, vn_ref[...])
    o_ref[...] = _diff_finish(o, g_ref, lam_init).astype(o_ref.dtype)


def _diff_attn_sample(dq, k_cache_t, v_cache, dk_new, dv_new, lam_p, sub_g, *, layer, batch, t, lam_init):
    past = v_cache.shape[2]
    new = pl.BlockSpec((t, LANES), lambda b, h: (b, h))
    return pl.pallas_call(
        functools.partial(_diff_sample_kernel, lam_init=lam_init),
        grid=(batch, DIFF_HEADS),
        in_specs=[new,
                  pl.BlockSpec((1, 1, 1, LANES, past), lambda b, h: (layer, b, h, 0, 0)),
                  pl.BlockSpec((1, 1, past, LANES), lambda b, h: (layer, b, 0, h)),
                  new, new,
                  pl.BlockSpec((4, DIFF_HEAD_DIM), lambda b, h: (0, 0)),
                  pl.BlockSpec((1, DIFF_V_DIM), lambda b, h: (0, 0))],
        out_specs=new,
        out_shape=jax.ShapeDtypeStruct((batch * t, DIFF_W), BF16),
        compiler_params=_cparams(("parallel", "parallel"), 40),
        name="diff_attn_sample",
    )(dq, k_cache_t, v_cache, dk_new, dv_new, lam_p, sub_g)


def _conv_kernel(cur_ref, prev_ref, wdw_ref, bdw_ref, lng_ref, lnb_ref, o_ref, xin_ref, y_ref, *, rc, nr):
    tm = cur_ref.shape[0]
    xin_ref[0:CONV_HALO, :] = prev_ref[0]
    xin_ref[CONV_HALO:CONV_HALO + tm, :] = cur_ref[...]
    xin_ref[CONV_HALO + tm:, :] = jnp.zeros((SUBLANES, D_MODEL), F32)
    first = CONV_HALO - CONV_STATE

    def taps(r, carry):
        r0 = pl.multiple_of(r * rc, rc)
        for c in range(D_MODEL // LANES):
            cl = slice(c * LANES, (c + 1) * LANES)
            y = jnp.zeros((rc, LANES), F32) + bdw_ref[:, cl]
            for s in range(SUBLANES):
                z = None
                for w in range(CONV_WIDTH):
                    if (first + w) % SUBLANES == s:
                        start = pl.multiple_of(r0 + (first + w - s), SUBLANES)
                        term = xin_ref[pl.ds(start, rc + SUBLANES), cl] * wdw_ref[w:w + 1, cl]
                        z = term if z is None else z + term
                y = y + z[s:s + rc]
            y_ref[pl.ds(r0, rc), cl] = y
        return carry

    lax.fori_loop(0, tm // rc, taps, 0)

    def norm(r, carry):
        r0 = pl.multiple_of(r * nr, nr)
        y = y_ref[pl.ds(r0, nr), :]
        mu = jnp.mean(y, axis=-1, keepdims=True)
        yc = y - mu
        var = jnp.mean(yc * yc, axis=-1, keepdims=True)
        yn = yc * lax.rsqrt(var + EPS) * lng_ref[...] + lnb_ref[...]
        o_ref[pl.ds(r0, nr), :] = (yn * jax.nn.sigmoid(yn)).astype(o_ref.dtype)
        return carry

    lax.fori_loop(0, tm // nr, norm, 0)


def _conv(glu, prev, w_dw, b_dw, ln_g, ln_b, *, tm):
    rows = glu.shape[0]
    vec = pl.BlockSpec((1, D_MODEL), lambda i: (0, 0))
    return pl.pallas_call(
        functools.partial(_conv_kernel, rc=64, nr=min(128, tm)),
        grid=(rows // tm,),
        in_specs=[pl.BlockSpec((tm, D_MODEL), lambda i: (i, 0)),
                  pl.BlockSpec((1, CONV_HALO, D_MODEL), lambda i: (i, 0, 0)),
                  pl.BlockSpec((CONV_WIDTH, D_MODEL), lambda i: (0, 0)),
                  vec, vec, vec],
        out_specs=pl.BlockSpec((tm, D_MODEL), lambda i: (i, 0)),
        out_shape=jax.ShapeDtypeStruct((rows, D_MODEL), BF16),
        scratch_shapes=[pltpu.VMEM((tm + CONV_HALO + SUBLANES, D_MODEL), F32), pltpu.VMEM((tm, D_MODEL), F32)],
        compiler_params=_cparams(("parallel",), 24),
        name="conv",
    )(glu, prev, w_dw, b_dw, ln_g, ln_b)


def _merge_kernel(x_ref, mod_ref, preg_ref, postg_ref, am_ref, ad_ref, yc_ref,
                  wbg_ref, bbg_ref, wmo_ref, wdo_ref, wpw_ref, bpw_ref, wout_ref, o_ref, *, nseq):
    x = x_ref[...]
    mod = mod_ref[...]
    u = _mod_norm(x, preg_ref[...], mod, nseq).astype(BF16)
    branches = (_dot(am_ref[...], wmo_ref[...]),
                _dot(ad_ref[...], wdo_ref[...]),
                _dot(yc_ref[...], wpw_ref[...]) + bpw_ref[...])
    merged = None
    for k, br in enumerate(branches):
        sl = slice(k * D_MODEL, (k + 1) * D_MODEL)
        gate = jax.nn.sigmoid(_dot(u, wbg_ref[:, sl]) + bbg_ref[:, sl])
        merged = gate * br if merged is None else merged + gate * br
    o = _dot(merged.astype(BF16), wout_ref[...])
    o_ref[...] = _gated_residual(x, o, postg_ref[...], mod, nseq, 1.0)


def _merge(x, mod, pre_g, post_g, a_mla, a_diff, y_conv, w_bg, b_bg, w_mo, w_do, w_pw, b_pw, w_out, *, srows, tm=512):
    rows = x.shape[0]
    nseq, mod_spec = _mod_spec(tm, srows)
    row = pl.BlockSpec((tm, D_MODEL), lambda i: (i, 0))
    sq = _const_spec((D_MODEL, D_MODEL))
    vec = _const_spec((1, D_MODEL))
    return pl.pallas_call(
        functools.partial(_merge_kernel, nseq=nseq),
        grid=(rows // tm,),
        in_specs=[row, mod_spec, vec, vec, row, row, row,
                  _const_spec((D_MODEL, 3 * D_MODEL)), _const_spec((1, 3 * D_MODEL)), sq, sq, sq, vec, sq],
        out_specs=row,
        out_shape=jax.ShapeDtypeStruct((rows, D_MODEL), F32),
        compiler_params=_cparams(("parallel",), 56),
        name="merge",
    )(x, mod, pre_g, post_g, a_mla, a_diff, y_conv, w_bg, b_bg, w_mo, w_do, w_pw, b_pw, w_out)


def _rope_angles(pos, rot_dim):
    half = rot_dim // 2
    inv_freq = ROPE_THETA ** (-jnp.arange(half, dtype=F32) / half)
    ang = pos.astype(F32)[:, None] * inv_freq[None, :]
    return jnp.cos(ang), jnp.sin(ang)


def _mla_tables(pos, scale):
    cos, sin = _rope_angles(pos, MLA_ROPE)
    t = pos.shape[0]
    z = lambda n: jnp.zeros((t, n), F32)
    pad = HEAD_SLOT - MLA_NOPE - MLA_ROPE
    c = jnp.concatenate([jnp.ones((t, MLA_NOPE), F32), cos, cos, z(pad)], axis=1)
    s1 = jnp.concatenate([z(MLA_NOPE), -sin, z(MLA_ROPE // 2), z(pad)], axis=1)
    s2 = jnp.concatenate([z(MLA_NOPE), z(MLA_ROPE // 2), sin, z(pad)], axis=1)
    return jnp.stack([c, s1, s2]) * scale


def _diff_tables(pos, scale):
    cos, sin = _rope_angles(pos, DIFF_HEAD_DIM)
    z = jnp.zeros_like(sin)
    c = jnp.concatenate([cos, cos, cos, cos], axis=1)
    s1 = jnp.concatenate([-sin, z, -sin, z], axis=1)
    s2 = jnp.concatenate([z, sin, z, sin], axis=1)
    return jnp.stack([c, s1, s2]) * scale


def _tables(pos, reps):
    tabs = (_mla_tables(pos, MLA_SCALE * LOG2E), _mla_tables(pos, 1.0),
            _diff_tables(pos, DIFF_SCALE * LOG2E), _diff_tables(pos, 1.0))
    return tuple(jnp.tile(t, (1, reps, 1)) for t in tabs)


def _pad_heads(w, heads, width):
    k = w.shape[0]
    w = w.reshape(k, heads, width)
    return jnp.pad(w, ((0, 0), (0, 0), (0, HEAD_SLOT - width))).reshape(k, heads * HEAD_SLOT)


def _pad_pairs(w):
    k = w.shape[0]
    w = w.reshape(k, MLA_HEADS // 2, 2, MLA_V)
    z = jnp.zeros_like(w[:, :, 0])
    lo = jnp.concatenate([w[:, :, 0], z], axis=-1)
    hi = jnp.concatenate([z, w[:, :, 1]], axis=-1)
    return jnp.stack([lo, hi], axis=2).reshape(k, MLA_PAD)


def _permute_w_in(w):
    s0 = MLA_Q_RANK + MLA_KV_RANK
    z = lambda n: jnp.zeros((w.shape[0], n), w.dtype)
    return jnp.concatenate([w[:, :s0], z(MLA_NOPE), w[:, s0:s0 + MLA_ROPE], z(HEAD_SLOT - MLA_NOPE - MLA_ROPE),
                            w[:, s0 + MLA_ROPE:]], axis=1)


def _conv_prev_prompt(glu, batch, seq, tm):
    g = glu.reshape(batch, seq // tm, tm, D_MODEL)[:, :-1, tm - CONV_HALO:, :]
    g = jnp.pad(g, ((0, 0), (1, 0), (0, 0), (0, 0)))
    return g.reshape(batch * (seq // tm), CONV_HALO, D_MODEL)


def kernel(x_prompt, x_sample, c_prompt, c_sample, cache_mla_ckv, cache_mla_kpe, cache_diff_k, cache_diff_v, state_conv, ada_w, ada_b, ffn1_pre_g, ffn1_post_g, ffn1_w_gate, ffn1_w_up, ffn1_w_down, mix_pre_g, mix_post_g, w_in, mla_q_norm_g, mla_w_uq, mla_kv_norm_g, mla_w_uk, mla_w_uv, mla_w_o, diff_lq1, diff_lk1, diff_lq2, diff_lk2, diff_subln_g, diff_w_o, conv_w_dw, conv_b_dw, conv_ln_g, conv_ln_b, conv_w_pw2, conv_b_pw2, w_branch_gate, b_branch_gate, w_out, ffn2_pre_g, ffn2_post_g, ffn2_w_gate, ffn2_w_up, ffn2_w_down):
    pb, pt, _ = x_prompt.shape
    sb, st, _ = x_sample.shape
    past = cache_mla_ckv.shape[2]
    assert st == CHUNK and past % CHUNK == 0, "sample rows must be exactly the chunk that follows the cache"

    mods = _ada(jnp.concatenate([c_prompt, c_sample], axis=0), ada_w, ada_b)
    mods = mods.reshape(DEPTH, pb + sb, 3, 3, D_MODEL)

    mix_tm = 256
    tabs_p = _tables(jnp.arange(pt, dtype=jnp.int32), 1)
    tabs_s = _tables(past + jnp.arange(st, dtype=jnp.int32), mix_tm // st)

    kpe_cache_t = jnp.swapaxes(cache_mla_kpe, 2, 3)
    dk_cache_t = jnp.transpose(cache_diff_k, (0, 1, 3, 4, 5, 2)).reshape(DEPTH, sb, DIFF_HEADS, DIFF_V_DIM, past)
    dv_cache = cache_diff_v.astype(BF16).reshape(DEPTH, sb, past, DIFF_W)

    xs = {"p": x_prompt.reshape(pb * pt, D_MODEL), "s": x_sample.reshape(sb * st, D_MODEL)}
    srows = {"p": pt, "s": st}
    states = {"p": [], "s": []}
    diff_state = None
    bf = lambda a: a.astype(BF16)
    vec = lambda a: a.reshape(1, -1)

    for l in range(DEPTH):
        lam_init = 0.8 - 0.6 * math.exp(-0.3 * l)
        w = dict(
            f1=(vec(ffn1_pre_g[l]), vec(ffn1_post_g[l]), bf(ffn1_w_gate[l]), bf(ffn1_w_up[l]), bf(ffn1_w_down[l])),
            f2=(vec(ffn2_pre_g[l]), vec(ffn2_post_g[l]), bf(ffn2_w_gate[l]), bf(ffn2_w_up[l]), bf(ffn2_w_down[l])),
            w_in=bf(_permute_w_in(w_in[l])),
            w_uq=bf(_pad_heads(mla_w_uq[l], MLA_HEADS, MLA_NOPE + MLA_ROPE)),
            w_uk=bf(_pad_heads(mla_w_uk[l], MLA_HEADS, MLA_NOPE)),
            w_uv=bf(mla_w_uv[l]),
            w_uv_pad=bf(_pad_pairs(mla_w_uv[l])),
            lam=jnp.stack([diff_lq1[l], diff_lk1[l], diff_lq2[l], diff_lk2[l]]),
        )
        for key in ("p", "s"):
            x = xs[key]
            sr = srows[key]
            mod = mods[l, :pb] if key == "p" else mods[l, pb:]
            x = _ffn(x, mod[:, 0], *w["f1"], srows=sr)
            outs = _mix_in(x, mod[:, 1], vec(mix_pre_g[l]), w["w_in"], vec(mla_q_norm_g[l]), w["w_uq"],
                           vec(mla_kv_norm_g[l]), w["w_uk"], w["w_uv"], tabs_p if key == "p" else tabs_s,
                           srows=sr, layer=l, state=diff_state if key == "p" else None, tm=mix_tm)
            q_pad, ckv, kpe, dq, dk_b, dv_b, glu = outs[:7]
            if key == "p":
                k_pad, v_pad, *diff_state = outs[7:]
                dk_f = dv_f = None
                a_mla = _mla_attn_prompt(q_pad, k_pad, v_pad, batch=pb, seq=pt)
                a_diff = _diff_attn_prompt(dq, dk_b, dv_b, w["lam"], vec(diff_subln_g[l]),
                                           batch=pb, seq=pt, lam_init=lam_init)
                conv_tm = 256
                prev = _conv_prev_prompt(glu, pb, pt, conv_tm)
                new_conv = glu.reshape(pb, pt, D_MODEL)[:, pt - CONV_STATE:, :]
                nb, nt = pb, pt
            else:
                dk_f, dv_f = outs[7:]
                a_mla = _mla_attn_sample(q_pad, cache_mla_ckv, kpe_cache_t, ckv, kpe, w["w_uk"],
                                         w["w_uv_pad"], layer=l, batch=sb, t=st)
                a_diff = _diff_attn_sample(dq, dk_cache_t, dv_cache, dk_b, dv_b, w["lam"],
                                           vec(diff_subln_g[l]), layer=l, batch=sb, t=st, lam_init=lam_init)
                conv_tm = st
                prev = jnp.pad(state_conv[l], ((0, 0), (CONV_HALO - CONV_STATE, 0), (0, 0)))
                xin = jnp.concatenate([state_conv[l], glu.reshape(sb, st, D_MODEL)], axis=1)
                new_conv = xin[:, st:, :]
                nb, nt = sb, st
            y_conv = _conv(glu, prev, conv_w_dw[l], vec(conv_b_dw[l]), vec(conv_ln_g[l]), vec(conv_ln_b[l]),
                           tm=conv_tm)
            x = _merge(x, mod[:, 1], vec(mix_pre_g[l]), vec(mix_post_g[l]), a_mla, a_diff, y_conv,
                       bf(w_branch_gate[l]), vec(b_branch_gate[l]), bf(mla_w_o[l]), bf(diff_w_o[l]),
                       bf(conv_w_pw2[l]), vec(conv_b_pw2[l]), bf(w_out[l]), srows=sr)
            x = _ffn(x, mod[:, 2], *w["f2"], srows=sr)
            xs[key] = x
            states[key].append((
                ckv.reshape(nb, nt, MLA_KV_RANK),
                kpe[:, MLA_NOPE:MLA_NOPE + MLA_ROPE].reshape(nb, nt, MLA_ROPE),
                None if dk_f is None else dk_f.reshape(nb, nt, DIFF_HEADS, 2, DIFF_HEAD_DIM),
                None if dv_f is None else dv_f.reshape(nb, nt, DIFF_HEADS, DIFF_V_DIM),
                new_conv))

    stack = lambda key, k: jnp.stack([s[k] for s in states[key]])
    dk_t, dv_s = diff_state
    dk_prompt = jnp.transpose(dk_t.reshape(DEPTH, pb, DIFF_HEADS, 2, DIFF_HEAD_DIM, pt), (0, 1, 5, 2, 3, 4))
    dv_prompt = dv_s.reshape(DEPTH, pb, pt, DIFF_HEADS, DIFF_V_DIM)
    return (xs["p"].reshape(pb, pt, D_MODEL), xs["s"].reshape(sb, st, D_MODEL),
            stack("p", 0), stack("p", 1), dk_prompt, dv_prompt, stack("p", 4),
            *(stack("s", k) for k in range(5)))
```

```python
import functools
import math

import jax
import jax.numpy as jnp
from jax import lax
from jax.experimental import pallas as pl
from jax.experimental.pallas import tpu as pltpu

D_MODEL = 1024
DEPTH = 2
CHUNK = 64
ROPE_THETA = 10000.0
EPS = 1e-6
MACARON_W = 0.5

MLA_HEADS = 16
MLA_NOPE = 64
MLA_ROPE = 32
MLA_V = 64
MLA_KV_RANK = 256
MLA_Q_RANK = 512
MLA_SCALE = (MLA_NOPE + MLA_ROPE) ** -0.5

DIFF_HEADS = 8
DIFF_HEAD_DIM = 64
DIFF_V_DIM = 2 * DIFF_HEAD_DIM
DIFF_SCALE = DIFF_HEAD_DIM ** -0.5
LOG2E = math.log2(math.e)

CONV_WIDTH = 31
CONV_STATE = CONV_WIDTH - 1
CONV_HALO = 32
D_FF = 2816
N_ADA = 9

LANES = 128
SUBLANES = 8
HEAD_SLOT = LANES
MLA_PAD = MLA_HEADS * HEAD_SLOT
MLA_ONE_LANE = (MLA_V, 0)
DIFF_W = DIFF_HEADS * 2 * DIFF_HEAD_DIM
MIB = 1024 * 1024

C_CQ = 0
C_CKV = C_CQ + MLA_Q_RANK
C_KPE = C_CKV + MLA_KV_RANK
C_DQ = C_KPE + HEAD_SLOT
C_DK = C_DQ + DIFF_W
C_DV = C_DK + DIFF_W
C_GA = C_DV + DIFF_W
C_GB = C_GA + D_MODEL
C_END = C_GB + D_MODEL

F32 = jnp.float32
BF16 = jnp.bfloat16


def _cparams(sem, vmem_mib):
    return pltpu.CompilerParams(dimension_semantics=sem, vmem_limit_bytes=int(vmem_mib * MIB))


def _dot(a, b):
    return jnp.dot(a, b, preferred_element_type=F32)


def _dot_nt(a, b):
    return lax.dot_general(a, b, (((1,), (1,)), ((), ())), preferred_element_type=F32)


def _rms(x):
    return x * lax.rsqrt(jnp.mean(x * x, axis=-1, keepdims=True) + EPS)


def _mod_norm(x, g, mod, nseq):
    tm = x.shape[0]
    xn = _rms(x) * g
    sh = mod[:, 0:1, :]
    sc = mod[:, 1:2, :]
    if nseq == 1:
        return xn * (1.0 + sc[0]) + sh[0]
    xn = xn.reshape(nseq, tm // nseq, x.shape[1])
    return (xn * (1.0 + sc) + sh).reshape(tm, x.shape[1])


def _gated_residual(x, y, post_g, mod, nseq, weight):
    tm = x.shape[0]
    yn = _rms(y) * post_g
    gate = mod[:, 2:3, :]
    if nseq == 1:
        return x + weight * gate[0] * yn
    yn = yn.reshape(nseq, tm // nseq, x.shape[1])
    return x + (weight * gate * yn).reshape(tm, x.shape[1])


def _mod_spec(tm, srows):
    if tm <= srows:
        per = srows // tm
        return 1, pl.BlockSpec((1, 3, D_MODEL), lambda *idx: (idx[0] // per, 0, 0))
    nseq = tm // srows
    return nseq, pl.BlockSpec((nseq, 3, D_MODEL), lambda *idx: (idx[0], 0, 0))


def _const_spec(shape):
    nd = len(shape)
    return pl.BlockSpec(shape, lambda *idx: (0,) * nd, pipeline_mode=pl.Buffered(1))


def _ada_kernel(c_ref, w_ref, b_ref, o_ref):
    c = c_ref[...]
    a = (c * jax.nn.sigmoid(c)).astype(BF16)
    o_ref[0] = _dot(a, w_ref[0].astype(BF16)) + b_ref[0]


def _ada(c_all, ada_w, ada_b):
    nb = c_all.shape[0]
    tn = 1024
    n = N_ADA * D_MODEL
    return pl.pallas_call(
        _ada_kernel,
        grid=(DEPTH, n // tn),
        in_specs=[pl.BlockSpec((nb, D_MODEL), lambda l, j: (0, 0)),
                  pl.BlockSpec((1, D_MODEL, tn), lambda l, j: (l, 0, j)),
                  pl.BlockSpec((1, 1, tn), lambda l, j: (l, 0, j))],
        out_specs=pl.BlockSpec((1, nb, tn), lambda l, j: (l, 0, j)),
        out_shape=jax.ShapeDtypeStruct((DEPTH, nb, n), F32),
        compiler_params=_cparams(("parallel", "parallel"), 24),
        name="ada",
    )(c_all, ada_w, ada_b.reshape(DEPTH, 1, n))


def _ffn_kernel(x_ref, mod_ref, preg_ref, postg_ref, wg_ref, wu_ref, wd_ref, o_ref, *, nseq):
    x = x_ref[...]
    mod = mod_ref[...]
    h = _mod_norm(x, preg_ref[...], mod, nseq).astype(BF16)
    g = _dot(h, wg_ref[...])
    u = _dot(h, wu_ref[...])
    a = (g * jax.nn.sigmoid(g) * u).astype(BF16)
    o_ref[...] = _gated_residual(x, _dot(a, wd_ref[...]), postg_ref[...], mod, nseq, MACARON_W)


def _ffn(x, mod, pre_g, post_g, wg, wu, wd, *, srows, tm=512):
    rows = x.shape[0]
    nseq, mod_spec = _mod_spec(tm, srows)
    row_spec = pl.BlockSpec((tm, D_MODEL), lambda i: (i, 0))
    vec_spec = _const_spec((1, D_MODEL))
    return pl.pallas_call(
        functools.partial(_ffn_kernel, nseq=nseq),
        grid=(rows // tm,),
        in_specs=[row_spec, mod_spec, vec_spec, vec_spec,
                  _const_spec((D_MODEL, D_FF)), _const_spec((D_MODEL, D_FF)), _const_spec((D_FF, D_MODEL))],
        out_specs=row_spec,
        out_shape=jax.ShapeDtypeStruct((rows, D_MODEL), F32),
        compiler_params=_cparams(("parallel",), 56),
        name="ffn",
    )(x, mod, pre_g, post_g, wg, wu, wd)


def _rope_store(x, tbl_ref, shift, out_refs, t_ref=None):
    n = x.shape[1]
    c, s1, s2 = tbl_ref[0], tbl_ref[1], tbl_ref[2]
    xm = pltpu.roll(x, n - shift, 1)
    xp = pltpu.roll(x, shift, 1)
    for b in range(n // LANES):
        sl = slice(b * LANES, (b + 1) * LANES)
        r = x[:, sl] * c + xm[:, sl] * s1 + xp[:, sl] * s2
        for ref in out_refs:
            ref[:, sl] = r.astype(ref.dtype)
        if t_ref is not None:
            t_ref[sl, :] = r.T


N_MIX_IN = 13


def _mix_in_kernel(*refs, nseq, prompt):
    (x_ref, mod_ref, preg_ref, win_ref, qg_ref, wuq_ref, kvg_ref, wuk_ref, wuv_ref,
     tmq_ref, tmk_ref, tdq_ref, tdk_ref) = refs[:N_MIX_IN]
    outs = refs[len(refs) - (11 if prompt else 9):]
    qpad_ref, ckv_ref, kpe_ref, dq_ref, dkb_ref, dvb_ref, glu_ref = outs[:7]
    if prompt:
        kpad_ref, vpad_ref, dkt_ref, dvs_ref = outs[7:]
        dk_outs, dk_t, dvf_ref = (dkb_ref,), dkt_ref.at[0, 0], dvs_ref.at[0]
    else:
        dkf_ref, dvf_ref = outs[7:]
        dk_outs, dk_t = (dkf_ref, dkb_ref), None
    u = _mod_norm(x_ref[...], preg_ref[...], mod_ref[...], nseq).astype(BF16)

    def proj(a, b):
        return _dot(u, win_ref[:, a:b])

    cqn = (_rms(proj(C_CQ, C_CKV)) * qg_ref[...]).astype(BF16)
    _rope_store(_dot(cqn, wuq_ref[...]), tmq_ref, MLA_ROPE // 2, (qpad_ref,))

    ckvn = _rms(proj(C_CKV, C_KPE)) * kvg_ref[...]
    ckv_ref[...] = ckvn
    _rope_store(proj(C_KPE, C_DQ), tmk_ref, MLA_ROPE // 2, (kpe_ref,))

    if prompt:
        ckvb = ckvn.astype(BF16)
        kpe = kpe_ref[...]
        kn = _dot(ckvb, wuk_ref[...])
        for h in range(MLA_HEADS):
            sl = slice(h * HEAD_SLOT, (h + 1) * HEAD_SLOT)
            kpad_ref[:, sl] = (kn[:, sl] + kpe).astype(BF16)
        v = _dot(ckvb, wuv_ref[...])
        lane = lax.broadcasted_iota(jnp.int32, (1, LANES), 1)
        low = lane < MLA_V
        one_even = (lane == MLA_ONE_LANE[0]).astype(F32)
        one_odd = (lane == MLA_ONE_LANE[1]).astype(F32)
        for p in range(MLA_HEADS // 2):
            vp = v[:, p * LANES:(p + 1) * LANES]
            vpad_ref[:, (2 * p) * LANES:(2 * p + 1) * LANES] = jnp.where(low, vp, one_even).astype(BF16)
            vpad_ref[:, (2 * p + 1) * LANES:(2 * p + 2) * LANES] = jnp.where(low, one_odd, vp).astype(BF16)

    _rope_store(proj(C_DQ, C_DK), tdq_ref, DIFF_HEAD_DIM // 2, (dq_ref,))
    _rope_store(proj(C_DK, C_DV), tdk_ref, DIFF_HEAD_DIM // 2, dk_outs, dk_t)
    dv = proj(C_DV, C_GA)
    dvf_ref[...] = dv
    dvb_ref[...] = dv.astype(BF16)

    glu_ref[...] = proj(C_GA, C_GB) * jax.nn.sigmoid(proj(C_GB, C_END))


def _mix_in(x, mod, pre_g, w_in, q_g, w_uq, kv_g, w_uk, w_uv, tables, *, srows, layer, state=None, tm=256):
    rows = x.shape[0]
    prompt = srows > tm
    nseq, mod_spec = _mod_spec(tm, srows)
    ntab = tables[0].shape[1] // tm
    row = lambda n: pl.BlockSpec((tm, n), lambda i: (i, 0))
    tab_spec = pl.BlockSpec((3, tm, LANES), lambda i: (0, i % ntab, 0))
    out_cols = [(MLA_PAD, BF16), (MLA_KV_RANK, F32), (LANES, F32), (DIFF_W, BF16), (DIFF_W, BF16), (DIFF_W, BF16),
                (D_MODEL, F32)]
    out_cols += [(MLA_PAD, BF16), (MLA_PAD, BF16)] if prompt else [(DIFF_W, F32), (DIFF_W, F32)]
    out_specs = [row(n) for n, _ in out_cols]
    out_shape = [jax.ShapeDtypeStruct((rows, n), dt) for n, dt in out_cols]
    in_specs = [row(D_MODEL), mod_spec, _const_spec((1, D_MODEL)), _const_spec((D_MODEL, C_END)),
                _const_spec((1, MLA_Q_RANK)), _const_spec((MLA_Q_RANK, MLA_PAD)),
                _const_spec((1, MLA_KV_RANK)), _const_spec((MLA_KV_RANK, MLA_PAD)),
                _const_spec((MLA_KV_RANK, MLA_HEADS * MLA_V)),
                tab_spec, tab_spec, tab_spec, tab_spec]
    operands = [x, mod, pre_g, w_in, q_g, w_uq, kv_g, w_uk, w_uv, *tables]
    assert len(operands) == N_MIX_IN
    aliases = {}
    if prompt:
        per = srows // tm
        out_specs += [pl.BlockSpec((1, 1, DIFF_W, tm), lambda i: (layer, i // per, 0, i % per)),
                      pl.BlockSpec((1, tm, DIFF_W), lambda i: (layer, i, 0))]
        out_shape += [jax.ShapeDtypeStruct((DEPTH, rows // srows, DIFF_W, srows), F32),
                      jax.ShapeDtypeStruct((DEPTH, rows, DIFF_W), F32)]
        if state is not None:
            operands += list(state)
            in_specs += [pl.BlockSpec(memory_space=pl.ANY)] * len(state)
            aliases = {N_MIX_IN + k: len(out_shape) - len(state) + k for k in range(len(state))}
    return pl.pallas_call(
        functools.partial(_mix_in_kernel, nseq=nseq, prompt=prompt),
        grid=(rows // tm,),
        in_specs=in_specs,
        out_specs=out_specs,
        out_shape=out_shape,
        input_output_aliases=aliases,
        compiler_params=_cparams(("parallel",), 56),
        name="mix_in",
    )(*operands)


ATTN_TQ = 2048
ATTN_TK = 512


def _rep_lanes(x, n):
    return jnp.concatenate([x] * n, axis=1) if n > 1 else x


def _online_step(s, m_ref, l_ref, rows):
    m_prev = m_ref[rows, :]
    m_new = jnp.maximum(m_prev, jnp.max(s, axis=-1, keepdims=True))
    alpha = jnp.exp2(m_prev - m_new)
    p = jnp.exp2((s - _rep_lanes(m_new, s.shape[1] // LANES)).astype(BF16))
    l_ref[rows, :] = alpha * l_ref[rows, :] + jnp.sum(p.astype(F32), axis=-1, keepdims=True)
    m_ref[rows, :] = m_new
    return p, alpha


def _causal_sweep(i, tq, tk, tile_fn):
    per = tq // tk

    def body(j, carry):
        tile_fn(j, 0, None)
        return carry

    lax.fori_loop(0, i * per, body, 0)
    for dj in range(per):
        tile_fn(i * per + dj, dj * tk, dj)


def _diag_mask(s, row0, dj, tk):
    qc = (lax.broadcasted_iota(jnp.int32, s.shape, 0) + row0) // CHUNK
    kc = (lax.broadcasted_iota(jnp.int32, s.shape, 1) + dj * tk) // CHUNK
    return jnp.where(kc <= qc, s, -jnp.inf)


def _mla_attn_kernel(q_ref, k_ref, v_ref, o_ref, m_ref, acc_ref, *, tq, tk):
    i = pl.program_id(2)
    m_ref[...] = jnp.full_like(m_ref, -jnp.inf)
    acc_ref[...] = jnp.zeros_like(acc_ref)

    def tile(j, row0, dj):
        keys = pl.ds(pl.multiple_of(j * tk, tk), tk)
        rows = pl.ds(row0, tq - row0)
        for hh in range(2):
            sl = slice(hh * HEAD_SLOT, (hh + 1) * HEAD_SLOT)
            s = _dot_nt(q_ref[rows, sl], k_ref[keys, sl])
            if dj is not None:
                s = _diag_mask(s, row0, dj, tk)
            m_prev = m_ref[hh, rows, :]
            m_new = jnp.maximum(m_prev, jnp.max(s, axis=-1, keepdims=True))
            p = jnp.exp2((s - _rep_lanes(m_new, tk // LANES)).astype(BF16))
            m_ref[hh, rows, :] = m_new
            acc_ref[hh, rows, :] = acc_ref[hh, rows, :] * jnp.exp2(m_prev - m_new) + _dot(p, v_ref[keys, sl])

    _causal_sweep(i, tq, tk, tile)
    low = lax.broadcasted_iota(jnp.int32, (1, LANES), 1) < MLA_V
    a0, a1 = acc_ref[0], acc_ref[1]
    o_ref[...] = jnp.where(low, a0 / a0[:, MLA_ONE_LANE[0]:MLA_ONE_LANE[0] + 1],
                           a1 / a1[:, MLA_ONE_LANE[1]:MLA_ONE_LANE[1] + 1]).astype(o_ref.dtype)


def _mla_attn_prompt(q_pad, k_pad, v_pad, *, batch, seq, tq=ATTN_TQ, tk=ATTN_TK):
    nq = seq // tq
    pair = 2 * HEAD_SLOT
    stat = pltpu.VMEM((2, tq, LANES), F32)
    return pl.pallas_call(
        functools.partial(_mla_attn_kernel, tq=tq, tk=tk),
        grid=(batch, MLA_HEADS // 2, nq),
        in_specs=[pl.BlockSpec((tq, pair), lambda b, p, i: (b * nq + i, p)),
                  pl.BlockSpec((seq, pair), lambda b, p, i: (b, p)),
                  pl.BlockSpec((seq, pair), lambda b, p, i: (b, p))],
        out_specs=pl.BlockSpec((tq, LANES), lambda b, p, i: (b * nq + i, p)),
        out_shape=jax.ShapeDtypeStruct((batch * seq, MLA_HEADS * MLA_V), BF16),
        scratch_shapes=[stat, stat],
        compiler_params=_cparams(("parallel", "parallel", "arbitrary"), 48),
        name="mla_attn",
    )(q_pad, k_pad, v_pad)


def _diff_lambda(lam_ref, lam_init):
    lp = lam_ref[...]
    s1 = jnp.sum(lp[0:1] * lp[1:2], axis=-1, keepdims=True)
    s2 = jnp.sum(lp[2:3] * lp[3:4], axis=-1, keepdims=True)
    return jnp.exp(s1) - jnp.exp(s2) + lam_init


def _diff_finish(o, g_ref, lam_init):
    return _rms(o) * g_ref[...] * (1.0 - lam_init)


def _split_maps(q):
    low = lax.broadcasted_iota(jnp.int32, (1, LANES), 1) < DIFF_HEAD_DIM
    zero = jnp.zeros_like(q)
    return jnp.where(low, q, zero), jnp.where(low, zero, q)


def _diff_attn_kernel(q_ref, k_ref, v_ref, lam_ref, g_ref, o_ref, qs_ref, m_ref, l_ref, acc_ref, *, tq, tk, lam_init):
    i = pl.program_id(2)
    m_ref[...] = jnp.full_like(m_ref, -jnp.inf)
    l_ref[...] = jnp.zeros_like(l_ref)
    acc_ref[...] = jnp.zeros_like(acc_ref)
    qs_ref[0], qs_ref[1] = _split_maps(q_ref[...])

    def tile(j, row0, dj):
        keys = pl.ds(pl.multiple_of(j * tk, tk), tk)
        rows = pl.ds(row0, tq - row0)
        k = k_ref[keys, :]
        v = v_ref[keys, :]
        for mp in range(2):
            s = _dot_nt(qs_ref[mp, rows, :], k)
            if dj is not None:
                s = _diag_mask(s, row0, dj, tk)
            p, alpha = _online_step(s, m_ref.at[mp], l_ref.at[mp], rows)
            acc_ref[mp, rows, :] = acc_ref[mp, rows, :] * alpha + _dot(p, v)

    _causal_sweep(i, tq, tk, tile)
    lam = _diff_lambda(lam_ref, lam_init)
    o = acc_ref[0] / l_ref[0] - lam * (acc_ref[1] / l_ref[1])
    o_ref[...] = _diff_finish(o, g_ref, lam_init).astype(o_ref.dtype)


def _diff_attn_prompt(dq, dk, dv, lam_p, sub_g, *, batch, seq, lam_init, tq=ATTN_TQ, tk=ATTN_TK):
    nq = seq // tq
    stat = pltpu.VMEM((2, tq, LANES), F32)
    return pl.pallas_call(
        functools.partial(_diff_attn_kernel, tq=tq, tk=tk, lam_init=lam_init),
        grid=(batch, DIFF_HEADS, nq),
        in_specs=[pl.BlockSpec((tq, LANES), lambda b, h, i: (b * nq + i, h)),
                  pl.BlockSpec((seq, LANES), lambda b, h, i: (b, h)),
                  pl.BlockSpec((seq, LANES), lambda b, h, i: (b, h)),
                  pl.BlockSpec((4, DIFF_HEAD_DIM), lambda b, h, i: (0, 0)),
                  pl.BlockSpec((1, DIFF_V_DIM), lambda b, h, i: (0, 0))],
        out_specs=pl.BlockSpec((tq, LANES), lambda b, h, i: (b * nq + i, h)),
        out_shape=jax.ShapeDtypeStruct((batch * seq, DIFF_W), BF16),
        scratch_shapes=[pltpu.VMEM((2, tq, LANES), BF16), stat, stat, stat],
        compiler_params=_cparams(("parallel", "parallel", "arbitrary"), 48),
        name="diff_attn",
    )(dq, dk, dv, lam_p, sub_g)


def _mla_sample_kernel(q_ref, ckvp_ref, kpet_ref, ckvn_ref, kpen_ref, wuk_ref, wuvp_ref, o_ref, *, group):
    ckv_p = ckvp_ref[0, 0].astype(BF16)
    past = ckv_p.shape[0]
    kpe_t = jnp.concatenate([kpet_ref[0, 0].astype(BF16), jnp.zeros((LANES - MLA_ROPE, past), BF16)], axis=0)
    ckv_n = ckvn_ref[...].astype(BF16)
    kpe_n = kpen_ref[...].astype(BF16)
    t = q_ref.shape[0]
    for g0 in range(0, MLA_HEADS, group):
        qlat, qfull, qpe = [], [], []
        for h in range(g0, g0 + group):
            sl = slice(h * HEAD_SLOT, (h + 1) * HEAD_SLOT)
            qh = q_ref[:, sl]
            qlat.append(_dot_nt(qh, wuk_ref[:, sl]).astype(BF16))
            qfull.append(qh)
            qpe.append(jnp.concatenate([qh[:, MLA_NOPE:MLA_NOPE + MLA_ROPE],
                                        jnp.zeros((t, LANES - MLA_ROPE), BF16)], axis=1))
        qlat = jnp.concatenate(qlat, axis=0)
        qfull = jnp.concatenate(qfull, axis=0)
        qpe = jnp.concatenate(qpe, axis=0)
        s_p = _dot_nt(qlat, ckv_p) + _dot(qpe, kpe_t)
        s_n = _dot_nt(qlat, ckv_n) + _dot_nt(qfull, kpe_n)
        m = jnp.maximum(jnp.max(s_p, axis=-1, keepdims=True), jnp.max(s_n, axis=-1, keepdims=True))
        p_p = jnp.exp2(s_p - m)
        p_n = jnp.exp2(s_n - m)
        l = jnp.sum(p_p, axis=-1, keepdims=True) + jnp.sum(p_n, axis=-1, keepdims=True)
        o_lat = ((_dot(p_p.astype(BF16), ckv_p) + _dot(p_n.astype(BF16), ckv_n)) / l).astype(BF16)
        for a in range(0, group, 2):
            h = g0 + a
            o_pair = (_dot(o_lat[a * t:(a + 1) * t], wuvp_ref[:, h * HEAD_SLOT:(h + 1) * HEAD_SLOT])
                      + _dot(o_lat[(a + 1) * t:(a + 2) * t], wuvp_ref[:, (h + 1) * HEAD_SLOT:(h + 2) * HEAD_SLOT]))
            o_ref[:, (h // 2) * LANES:(h // 2 + 1) * LANES] = o_pair.astype(o_ref.dtype)


def _mla_attn_sample(q_pad, ckv_cache, kpe_cache_t, ckv_new, kpe_new, w_uk, w_uv_pad, *, layer, batch, t, group=4):
    past = ckv_cache.shape[2]
    return pl.pallas_call(
        functools.partial(_mla_sample_kernel, group=group),
        grid=(batch,),
        in_specs=[pl.BlockSpec((t, MLA_PAD), lambda b: (b, 0)),
                  pl.BlockSpec((1, 1, past, MLA_KV_RANK), lambda b: (layer, b, 0, 0)),
                  pl.BlockSpec((1, 1, MLA_ROPE, past), lambda b: (layer, b, 0, 0)),
                  pl.BlockSpec((t, MLA_KV_RANK), lambda b: (b, 0)),
                  pl.BlockSpec((t, LANES), lambda b: (b, 0)),
                  _const_spec((MLA_KV_RANK, MLA_PAD)),
                  _const_spec((MLA_KV_RANK, MLA_PAD))],
        out_specs=pl.BlockSpec((t, MLA_HEADS * MLA_V), lambda b: (b, 0)),
        out_shape=jax.ShapeDtypeStruct((batch * t, MLA_HEADS * MLA_V), BF16),
        compiler_params=_cparams(("parallel",), 48),
        name="mla_attn_sample",
    )(q_pad, ckv_cache, kpe_cache_t, ckv_new, kpe_new, w_uk, w_uv_pad)


def _diff_sample_kernel(q_ref, kt_ref, v_ref, kn_ref, vn_ref, lam_ref, g_ref, o_ref, m_ref, l_ref, acc_ref,
                        *, tk, lam_init):
    kt_i = pl.program_id(1)
    t = q_ref.shape[0]

    @pl.when(kt_i == 0)
    def _():
        m_ref[...] = jnp.full_like(m_ref, -jnp.inf)
        l_ref[...] = jnp.zeros_like(l_ref)
        acc_ref[...] = jnp.zeros_like(acc_ref)

    def stacked_q(h):
        q0, q1 = _split_maps(q_ref[:, h * LANES:(h + 1) * LANES])
        return jnp.concatenate([q0, q1], axis=0)

    def update(h, s, v):
        m_prev = m_ref[h]
        m_new = jnp.maximum(m_prev, jnp.max(s, axis=-1, keepdims=True))
        alpha = jnp.exp2(m_prev - m_new)
        n = s.shape[1]
        p = jnp.exp2(s - (_rep_lanes(m_new, n // LANES) if n % LANES == 0 else m_new[:, :n]))
        l_ref[h] = alpha * l_ref[h] + jnp.sum(p, axis=-1, keepdims=True)
        m_ref[h] = m_new
        acc_ref[h] = acc_ref[h] * alpha + _dot(p.astype(BF16), v)

    for h in range(DIFF_HEADS):
        s = _dot(stacked_q(h), kt_ref[0, 0, h].astype(BF16))
        update(h, s, v_ref[0, 0, pl.ds(h, tk, stride=DIFF_HEADS), :].astype(BF16))

    @pl.when(kt_i == pl.num_programs(1) - 1)
    def _():
        lam = _diff_lambda(lam_ref, lam_init)
        for h in range(DIFF_HEADS):
            sl = slice(h * LANES, (h + 1) * LANES)
            update(h, _dot_nt(stacked_q(h), kn_ref[:, sl]), vn_ref[:, sl])
            a = acc_ref[h] / l_ref[h]
            o_ref[:, sl] = _diff_finish(a[:t] - lam * a[t:], g_ref, lam_init).astype(o_ref.dtype)


def _diff_attn_sample(dq, k_cache_t, v_cache, dk_new, dv_new, lam_p, sub_g, *, layer, batch, t, lam_init, tk=1024):
    past = k_cache_t.shape[4]
    row = pl.BlockSpec((t, DIFF_W), lambda b, k: (b, 0))
    stat = pltpu.VMEM((DIFF_HEADS, 2 * t, LANES), F32)
    return pl.pallas_call(
        functools.partial(_diff_sample_kernel, tk=tk, lam_init=lam_init),
        grid=(batch, past // tk),
        in_specs=[row,
                  pl.BlockSpec((1, 1, DIFF_HEADS, LANES, tk), lambda b, k: (layer, b, 0, 0, k)),
                  pl.BlockSpec((1, 1, tk * DIFF_HEADS, LANES), lambda b, k: (layer, b, k, 0)),
                  row, row,
                  pl.BlockSpec((4, DIFF_HEAD_DIM), lambda b, k: (0, 0)),
                  pl.BlockSpec((1, DIFF_V_DIM), lambda b, k: (0, 0))],
        out_specs=row,
        out_shape=jax.ShapeDtypeStruct((batch * t, DIFF_W), BF16),
        scratch_shapes=[stat, stat, stat],
        compiler_params=_cparams(("parallel", "arbitrary"), 48),
        name="diff_attn_sample",
    )(dq, k_cache_t, v_cache, dk_new, dv_new, lam_p, sub_g)


def _conv_kernel(cur_ref, prev_ref, wdw_ref, bdw_ref, lng_ref, lnb_ref, o_ref, xin_ref, y_ref, *, rc, nr):
    tm = cur_ref.shape[0]
    xin_ref[0:CONV_HALO, :] = prev_ref[0]
    xin_ref[CONV_HALO:CONV_HALO + tm, :] = cur_ref[...]
    xin_ref[CONV_HALO + tm:, :] = jnp.zeros((SUBLANES, D_MODEL), F32)
    first = CONV_HALO - CONV_STATE

    def taps(r, carry):
        r0 = pl.multiple_of(r * rc, rc)
        for c in range(D_MODEL // LANES):
            cl = slice(c * LANES, (c + 1) * LANES)
            y = jnp.zeros((rc, LANES), F32) + bdw_ref[:, cl]
            for s in range(SUBLANES):
                z = None
                for w in range(CONV_WIDTH):
                    if (first + w) % SUBLANES == s:
                        start = pl.multiple_of(r0 + (first + w - s), SUBLANES)
                        term = xin_ref[pl.ds(start, rc + SUBLANES), cl] * wdw_ref[w:w + 1, cl]
                        z = term if z is None else z + term
                y = y + z[s:s + rc]
            y_ref[pl.ds(r0, rc), cl] = y
        return carry

    lax.fori_loop(0, tm // rc, taps, 0)

    def norm(r, carry):
        r0 = pl.multiple_of(r * nr, nr)
        y = y_ref[pl.ds(r0, nr), :]
        mu = jnp.mean(y, axis=-1, keepdims=True)
        yc = y - mu
        var = jnp.mean(yc * yc, axis=-1, keepdims=True)
        yn = yc * lax.rsqrt(var + EPS) * lng_ref[...] + lnb_ref[...]
        o_ref[pl.ds(r0, nr), :] = (yn * jax.nn.sigmoid(yn)).astype(o_ref.dtype)
        return carry

    lax.fori_loop(0, tm // nr, norm, 0)


def _conv(glu, prev, w_dw, b_dw, ln_g, ln_b, *, tm):
    rows = glu.shape[0]
    vec = pl.BlockSpec((1, D_MODEL), lambda i: (0, 0))
    return pl.pallas_call(
        functools.partial(_conv_kernel, rc=64, nr=min(128, tm)),
        grid=(rows // tm,),
        in_specs=[pl.BlockSpec((tm, D_MODEL), lambda i: (i, 0)),
                  pl.BlockSpec((1, CONV_HALO, D_MODEL), lambda i: (i, 0, 0)),
                  pl.BlockSpec((CONV_WIDTH, D_MODEL), lambda i: (0, 0)),
                  vec, vec, vec],
        out_specs=pl.BlockSpec((tm, D_MODEL), lambda i: (i, 0)),
        out_shape=jax.ShapeDtypeStruct((rows, D_MODEL), BF16),
        scratch_shapes=[pltpu.VMEM((tm + CONV_HALO + SUBLANES, D_MODEL), F32), pltpu.VMEM((tm, D_MODEL), F32)],
        compiler_params=_cparams(("parallel",), 24),
        name="conv",
    )(glu, prev, w_dw, b_dw, ln_g, ln_b)


def _merge_kernel(x_ref, mod_ref, preg_ref, postg_ref, am_ref, ad_ref, yc_ref,
                  wbg_ref, bbg_ref, wmo_ref, wdo_ref, wpw_ref, bpw_ref, wout_ref, o_ref, *, nseq):
    x = x_ref[...]
    mod = mod_ref[...]
    u = _mod_norm(x, preg_ref[...], mod, nseq).astype(BF16)
    branches = (_dot(am_ref[...], wmo_ref[...]),
                _dot(ad_ref[...], wdo_ref[...]),
                _dot(yc_ref[...], wpw_ref[...]) + bpw_ref[...])
    merged = None
    for k, br in enumerate(branches):
        sl = slice(k * D_MODEL, (k + 1) * D_MODEL)
        gate = jax.nn.sigmoid(_dot(u, wbg_ref[:, sl]) + bbg_ref[:, sl])
        merged = gate * br if merged is None else merged + gate * br
    o = _dot(merged.astype(BF16), wout_ref[...])
    o_ref[...] = _gated_residual(x, o, postg_ref[...], mod, nseq, 1.0)


def _merge(x, mod, pre_g, post_g, a_mla, a_diff, y_conv, w_bg, b_bg, w_mo, w_do, w_pw, b_pw, w_out, *, srows, tm=512):
    rows = x.shape[0]
    nseq, mod_spec = _mod_spec(tm, srows)
    row = pl.BlockSpec((tm, D_MODEL), lambda i: (i, 0))
    sq = _const_spec((D_MODEL, D_MODEL))
    vec = _const_spec((1, D_MODEL))
    return pl.pallas_call(
        functools.partial(_merge_kernel, nseq=nseq),
        grid=(rows // tm,),
        in_specs=[row, mod_spec, vec, vec, row, row, row,
                  _const_spec((D_MODEL, 3 * D_MODEL)), _const_spec((1, 3 * D_MODEL)), sq, sq, sq, vec, sq],
        out_specs=row,
        out_shape=jax.ShapeDtypeStruct((rows, D_MODEL), F32),
        compiler_params=_cparams(("parallel",), 56),
        name="merge",
    )(x, mod, pre_g, post_g, a_mla, a_diff, y_conv, w_bg, b_bg, w_mo, w_do, w_pw, b_pw, w_out)


def _rope_angles(pos, rot_dim):
    half = rot_dim // 2
    inv_freq = ROPE_THETA ** (-jnp.arange(half, dtype=F32) / half)
    ang = pos.astype(F32)[:, None] * inv_freq[None, :]
    return jnp.cos(ang), jnp.sin(ang)


def _mla_tables(pos, scale):
    cos, sin = _rope_angles(pos, MLA_ROPE)
    t = pos.shape[0]
    z = lambda n: jnp.zeros((t, n), F32)
    pad = HEAD_SLOT - MLA_NOPE - MLA_ROPE
    c = jnp.concatenate([jnp.ones((t, MLA_NOPE), F32), cos, cos, z(pad)], axis=1)
    s1 = jnp.concatenate([z(MLA_NOPE), -sin, z(MLA_ROPE // 2), z(pad)], axis=1)
    s2 = jnp.concatenate([z(MLA_NOPE), z(MLA_ROPE // 2), sin, z(pad)], axis=1)
    return jnp.stack([c, s1, s2]) * scale


def _diff_tables(pos, scale):
    cos, sin = _rope_angles(pos, DIFF_HEAD_DIM)
    z = jnp.zeros_like(sin)
    c = jnp.concatenate([cos, cos, cos, cos], axis=1)
    s1 = jnp.concatenate([-sin, z, -sin, z], axis=1)
    s2 = jnp.concatenate([z, sin, z, sin], axis=1)
    return jnp.stack([c, s1, s2]) * scale


def _tables(pos, reps):
    tabs = (_mla_tables(pos, MLA_SCALE * LOG2E), _mla_tables(pos, 1.0),
            _diff_tables(pos, DIFF_SCALE * LOG2E), _diff_tables(pos, 1.0))
    return tuple(jnp.tile(t, (1, reps, 1)) for t in tabs)


def _pad_heads(w, heads, width):
    k = w.shape[0]
    w = w.reshape(k, heads, width)
    return jnp.pad(w, ((0, 0), (0, 0), (0, HEAD_SLOT - width))).reshape(k, heads * HEAD_SLOT)


def _pad_pairs(w):
    k = w.shape[0]
    w = w.reshape(k, MLA_HEADS // 2, 2, MLA_V)
    z = jnp.zeros_like(w[:, :, 0])
    lo = jnp.concatenate([w[:, :, 0], z], axis=-1)
    hi = jnp.concatenate([z, w[:, :, 1]], axis=-1)
    return jnp.stack([lo, hi], axis=2).reshape(k, MLA_PAD)


def _permute_w_in(w):
    s0 = MLA_Q_RANK + MLA_KV_RANK
    z = lambda n: jnp.zeros((w.shape[0], n), w.dtype)
    return jnp.concatenate([w[:, :s0], z(MLA_NOPE), w[:, s0:s0 + MLA_ROPE], z(HEAD_SLOT - MLA_NOPE - MLA_ROPE),
                            w[:, s0 + MLA_ROPE:]], axis=1)


def _conv_prev_prompt(glu, batch, seq, tm):
    g = glu.reshape(batch, seq // tm, tm, D_MODEL)[:, :-1, tm - CONV_HALO:, :]
    g = jnp.pad(g, ((0, 0), (1, 0), (0, 0), (0, 0)))
    return g.reshape(batch * (seq // tm), CONV_HALO, D_MODEL)


def kernel(x_prompt, x_sample, c_prompt, c_sample, cache_mla_ckv, cache_mla_kpe, cache_diff_k, cache_diff_v, state_conv, ada_w, ada_b, ffn1_pre_g, ffn1_post_g, ffn1_w_gate, ffn1_w_up, ffn1_w_down, mix_pre_g, mix_post_g, w_in, mla_q_norm_g, mla_w_uq, mla_kv_norm_g, mla_w_uk, mla_w_uv, mla_w_o, diff_lq1, diff_lk1, diff_lq2, diff_lk2, diff_subln_g, diff_w_o, conv_w_dw, conv_b_dw, conv_ln_g, conv_ln_b, conv_w_pw2, conv_b_pw2, w_branch_gate, b_branch_gate, w_out, ffn2_pre_g, ffn2_post_g, ffn2_w_gate, ffn2_w_up, ffn2_w_down):
    pb, pt, _ = x_prompt.shape
    sb, st, _ = x_sample.shape
    past = cache_mla_ckv.shape[2]
    assert st == CHUNK and past % CHUNK == 0, "sample rows must be exactly the chunk that follows the cache"

    mods = _ada(jnp.concatenate([c_prompt, c_sample], axis=0), ada_w, ada_b)
    mods = mods.reshape(DEPTH, pb + sb, 3, 3, D_MODEL)

    mix_tm = 256
    tabs_p = _tables(jnp.arange(pt, dtype=jnp.int32), 1)
    tabs_s = _tables(past + jnp.arange(st, dtype=jnp.int32), mix_tm // st)

    kpe_cache_t = jnp.swapaxes(cache_mla_kpe, 2, 3)
    dk_cache_t = jnp.transpose(cache_diff_k, (0, 1, 3, 4, 5, 2)).reshape(DEPTH, sb, DIFF_HEADS, DIFF_V_DIM, past)
    dv_cache = cache_diff_v.reshape(DEPTH, sb, past * DIFF_HEADS, DIFF_V_DIM)

    xs = {"p": x_prompt.reshape(pb * pt, D_MODEL), "s": x_sample.reshape(sb * st, D_MODEL)}
    srows = {"p": pt, "s": st}
    states = {"p": [], "s": []}
    diff_state = None
    bf = lambda a: a.astype(BF16)
    vec = lambda a: a.reshape(1, -1)

    for l in range(DEPTH):
        lam_init = 0.8 - 0.6 * math.exp(-0.3 * l)
        w = dict(
            f1=(vec(ffn1_pre_g[l]), vec(ffn1_post_g[l]), bf(ffn1_w_gate[l]), bf(ffn1_w_up[l]), bf(ffn1_w_down[l])),
            f2=(vec(ffn2_pre_g[l]), vec(ffn2_post_g[l]), bf(ffn2_w_gate[l]), bf(ffn2_w_up[l]), bf(ffn2_w_down[l])),
            w_in=bf(_permute_w_in(w_in[l])),
            w_uq=bf(_pad_heads(mla_w_uq[l], MLA_HEADS, MLA_NOPE + MLA_ROPE)),
            w_uk=bf(_pad_heads(mla_w_uk[l], MLA_HEADS, MLA_NOPE)),
            w_uv=bf(mla_w_uv[l]),
            w_uv_pad=bf(_pad_pairs(mla_w_uv[l])),
            lam=jnp.stack([diff_lq1[l], diff_lk1[l], diff_lq2[l], diff_lk2[l]]),
        )
        for key in ("p", "s"):
            x = xs[key]
            sr = srows[key]
            mod = mods[l, :pb] if key == "p" else mods[l, pb:]
            x = _ffn(x, mod[:, 0], *w["f1"], srows=sr)
            outs = _mix_in(x, mod[:, 1], vec(mix_pre_g[l]), w["w_in"], vec(mla_q_norm_g[l]), w["w_uq"],
                           vec(mla_kv_norm_g[l]), w["w_uk"], w["w_uv"], tabs_p if key == "p" else tabs_s,
                           srows=sr, layer=l, state=diff_state if key == "p" else None, tm=mix_tm)
            q_pad, ckv, kpe, dq, dk_b, dv_b, glu = outs[:7]
            if key == "p":
                k_pad, v_pad, *diff_state = outs[7:]
                dk_f = dv_f = None
                a_mla = _mla_attn_prompt(q_pad, k_pad, v_pad, batch=pb, seq=pt)
                a_diff = _diff_attn_prompt(dq, dk_b, dv_b, w["lam"], vec(diff_subln_g[l]),
                                           batch=pb, seq=pt, lam_init=lam_init)
                conv_tm = 256
                prev = _conv_prev_prompt(glu, pb, pt, conv_tm)
                new_conv = glu.reshape(pb, pt, D_MODEL)[:, pt - CONV_STATE:, :]
                nb, nt = pb, pt
            else:
                dk_f, dv_f = outs[7:]
                a_mla = _mla_attn_sample(q_pad, cache_mla_ckv, kpe_cache_t, ckv, kpe, w["w_uk"],
                                         w["w_uv_pad"], layer=l, batch=sb, t=st)
                a_diff = _diff_attn_sample(dq, dk_cache_t, dv_cache, dk_b, dv_b, w["lam"],
                                           vec(diff_subln_g[l]), layer=l, batch=sb, t=st, lam_init=lam_init)
                conv_tm = st
                prev = jnp.pad(state_conv[l], ((0, 0), (CONV_HALO - CONV_STATE, 0), (0, 0)))
                xin = jnp.concatenate([state_conv[l], glu.reshape(sb, st, D_MODEL)], axis=1)
                new_conv = xin[:, st:, :]
                nb, nt = sb, st
            y_conv = _conv(glu, prev, conv_w_dw[l], vec(conv_b_dw[l]), vec(conv_ln_g[l]), vec(conv_ln_b[l]),
                           tm=conv_tm)
            x = _merge(x, mod[:, 1], vec(mix_pre_g[l]), vec(mix_post_g[l]), a_mla, a_diff, y_conv,
                       bf(w_branch_gate[l]), vec(b_branch_gate[l]), bf(mla_w_o[l]), bf(diff_w_o[l]),
                       bf(conv_w_pw2[l]), vec(conv_b_pw2[l]), bf(w_out[l]), srows=sr)
            x = _ffn(x, mod[:, 2], *w["f2"], srows=sr)
            xs[key] = x
            states[key].append((
                ckv.reshape(nb, nt, MLA_KV_RANK),
                kpe[:, MLA_NOPE:MLA_NOPE + MLA_ROPE].reshape(nb, nt, MLA_ROPE),
                None if dk_f is None else dk_f.reshape(nb, nt, DIFF_HEADS, 2, DIFF_HEAD_DIM),
                None if dv_f is None else dv_f.reshape(nb, nt, DIFF_HEADS, DIFF_V_DIM),
                new_conv))

    stack = lambda key, k: jnp.stack([s[k] for s in states[key]])
    dk_t, dv_s = diff_state
    dk_prompt = jnp.transpose(dk_t.reshape(DEPTH, pb, DIFF_HEADS, 2, DIFF_HEAD_DIM, pt), (0, 1, 5, 2, 3, 4))
    dv_prompt = dv_s.reshape(DEPTH, pb, pt, DIFF_HEADS, DIFF_V_DIM)
    return (xs["p"].reshape(pb, pt, D_MODEL), xs["s"].reshape(sb, st, D_MODEL),
            stack("p", 0), stack("p", 1), dk_prompt, dv_prompt, stack("p", 4),
            *(stack("s", k) for k in range(5)))
```
